```python
import math
import jax
import jax.numpy as jnp
from jax import lax
import numpy as np

D_MODEL = 1024
BATCH = 16
SEQ = 2048
DEPTH = 4
DEC_BATCH = 8
DEC_SEQ = 64
PAST_LEN = 1024

CHUNK = 64
D_INNER = 2 * D_MODEL
D_SSD = D_INNER // 2
D_S5 = D_INNER - D_SSD
SSD_HEADDIM = 64
SSD_HEADS = D_SSD // SSD_HEADDIM
SSD_GROUPS = 4
SSD_STATE = 128
SSD_CONV = 4
SSD_CONV_DIM = D_SSD + 2 * SSD_GROUPS * SSD_STATE
S5_GROUP_CH = 16
S5_GROUPS = D_S5 // S5_GROUP_CH
S5_STATE = 64
IN_COLS = D_SSD + SSD_CONV_DIM + SSD_HEADS + 2 * D_S5
IN_SPLITS = (D_SSD, D_SSD + SSD_CONV_DIM, D_SSD + SSD_CONV_DIM + SSD_HEADS,
             D_SSD + SSD_CONV_DIM + SSD_HEADS + D_S5)
EPS = 1e-6

kernel_name = 'hymba_ssd_s5_stream_step'


def rmsnorm(x, g):
    xf = x.astype(jnp.float32)
    y = xf * lax.rsqrt(jnp.mean(xf * xf, axis=-1, keepdims=True) + EPS)
    return (y * g.astype(jnp.float32)).astype(x.dtype)


def block_len(length):
    return CHUNK if length % CHUNK == 0 else length


def to_blocks(t, q):
    b, length = t.shape[0], t.shape[1]
    return jnp.moveaxis(t.reshape((b, length // q, q) + t.shape[2:]), 1, 0)


def from_blocks(t):
    t = jnp.moveaxis(t, 0, 1)
    return t.reshape((t.shape[0], t.shape[1] * t.shape[2]) + t.shape[3:])


def causal_dwconv(x, prev, w, bias):
    length = x.shape[1]
    xp = jnp.concatenate([prev.astype(x.dtype), x], axis=1)
    y = bias.astype(x.dtype)
    for k in range(SSD_CONV):
        y = y + w[k] * xp[:, k:k + length]
    return y, xp[:, xp.shape[1] - (SSD_CONV - 1):]


def segsum(a):
    t = a.shape[-1]
    ae = jnp.broadcast_to(a[..., :, None], a.shape + (t,))
    strict = jnp.tril(jnp.ones((t, t), dtype=bool), -1)
    cs = jnp.cumsum(jnp.where(strict, ae, 0.0), axis=-2)
    return jnp.where(jnp.tril(jnp.ones((t, t), dtype=bool)), cs, -jnp.inf)


def ssd_scan(xdt, a, bm, cm, h0):
    b, length, n_h, p = xdt.shape
    g, n = bm.shape[2], bm.shape[3]
    r = n_h // g
    q = block_len(length)
    xb = to_blocks(xdt.reshape(b, length, g, r, p), q)
    ab = to_blocks(a.reshape(b, length, g, r), q)
    bb = to_blocks(bm, q)
    cbk = to_blocks(cm, q)

    def block(h, inp):
        x_q, a_q, b_q, c_q = inp
        a_t = jnp.moveaxis(a_q, 1, -1)
        a_cs = jnp.cumsum(a_t, axis=-1)
        decay_in = jnp.exp(segsum(a_t))
        cb = jnp.einsum('blgn,bsgn->bgls', c_q, b_q)
        y = jnp.einsum('bgrls,bsgrp->blgrp', cb[:, :, None] * decay_in, x_q)
        from_h = jnp.moveaxis(jnp.exp(a_cs), -1, 1)[..., None]
        y = y + jnp.einsum('blgn,bgrpn->blgrp', c_q, h) * from_h
        to_end = jnp.moveaxis(jnp.exp(a_cs[..., -1:] - a_cs), -1, 1)[..., None]
        h_new = (h * jnp.exp(a_cs[..., -1])[..., None, None]
                 + jnp.einsum('bsgn,bsgrp->bgrpn', b_q, x_q * to_end))
        return h_new, y

    h_fin, ys = lax.scan(block, h0.reshape(b, g, r, p, n), (xb, ab, bb, cbk))
    return from_blocks(ys).reshape(b, length, n_h, p), h_fin.reshape(b, n_h, p, n)


def ssd_branch(z, xbc, dt_raw, conv_prev, h0, conv_w, conv_b, dt_bias, a_log, d_ssd, ssd_norm_g):
    f32 = jnp.float32
    b, length, _ = xbc.shape
    xbc_c, conv_new = causal_dwconv(xbc, conv_prev, conv_w, conv_b)
    xbc_c = jax.nn.silu(xbc_c.astype(f32))
    xs, bm, cm = jnp.split(xbc_c, (D_SSD, D_SSD + SSD_GROUPS * SSD_STATE), axis=-1)
    xs = xs.reshape(b, length, SSD_HEADS, SSD_HEADDIM)
    bm = bm.reshape(b, length, SSD_GROUPS, SSD_STATE)
    cm = cm.reshape(b, length, SSD_GROUPS, SSD_STATE)
    dt = jax.nn.softplus(dt_raw.astype(f32) + dt_bias.astype(f32))
    a_neg = -jnp.exp(a_log.astype(f32))
    y, h_new = ssd_scan(xs * dt[..., None], dt * a_neg, bm, cm, h0.astype(f32))
    y = y + d_ssd.astype(f32)[:, None] * xs
    y = y.reshape(b, length, D_SSD) * jax.nn.silu(z.astype(f32))
    y = rmsnorm(y.reshape(b, length, SSD_GROUPS, D_SSD // SSD_GROUPS),
                ssd_norm_g.reshape(SSD_GROUPS, D_SSD // SSD_GROUPS)).reshape(b, length, D_SSD)
    return y.astype(z.dtype), conv_new, h_new


def _lin_op(e1, e2):
    a1, b1 = e1
    a2, b2 = e2
    return a1 * a2, a2 * b1 + b2


def s5_branch(u, z, h0_re, h0_im, lam_re, lam_im, log_dt, b_re, b_im, c_re, c_im, d_s5, w_glu, b_glu):
    f32 = jnp.float32
    bsz, length, _ = u.shape
    uf = u.astype(f32).reshape(bsz, length, S5_GROUPS, S5_GROUP_CH)
    lam = lax.complex(lam_re.astype(f32), lam_im.astype(f32))
    lam_bar = jnp.exp(lam * jnp.exp(log_dt.astype(f32))[:, None])
    b_bar = ((lam_bar - 1.0) / lam)[..., None] * lax.complex(b_re.astype(f32), b_im.astype(f32))
    c_mat = lax.complex(c_re.astype(f32), c_im.astype(f32))
    q = block_len(length)
    a_blk = jnp.broadcast_to(lam_bar, (bsz, q, S5_GROUPS, S5_STATE))

    def block(h, u_q):
        bu = jnp.einsum('gpc,bqgc->bqgp', b_bar, u_q.astype(jnp.complex64))
        a_cum, h_loc = lax.associative_scan(_lin_op, (a_blk, bu), axis=1)
        hs = a_cum * h[:, None] + h_loc
        return hs[:, -1], jnp.real(jnp.einsum('gcp,bqgp->bqgc', c_mat, hs))

    h0 = lax.complex(h0_re.astype(f32), h0_im.astype(f32))
    h_fin, ys = lax.scan(block, h0, to_blocks(uf, q))
    y = from_blocks(ys) + d_s5.astype(f32).reshape(S5_GROUPS, S5_GROUP_CH) * uf
    y = jax.nn.gelu(y.reshape(bsz, length, D_S5))
    y = y * jax.nn.sigmoid(y @ w_glu.astype(f32) + b_glu.astype(f32))
    y = y * jax.nn.silu(z.astype(f32))
    return y.astype(u.dtype), jnp.real(h_fin), jnp.imag(h_fin)


def mixer_layer(x, conv_prev, ssd_h0, s5_h0_re, s5_h0_im,
                norm_g, w_in, conv_w, conv_b, dt_bias, a_log, d_ssd, ssd_norm_g,
                lam_re, lam_im, log_dt, b_re, b_im, c_re, c_im, d_s5, w_glu, b_glu, w_out):
    h = rmsnorm(x, norm_g)
    z_a, xbc, dt_raw, u_b, z_b = jnp.split(h @ w_in, IN_SPLITS, axis=-1)
    y_a, conv_new, ssd_new = ssd_branch(z_a, xbc, dt_raw, conv_prev, ssd_h0, conv_w, conv_b,
                                        dt_bias, a_log, d_ssd, ssd_norm_g)
    y_b, s5_re, s5_im = s5_branch(u_b, z_b, s5_h0_re, s5_h0_im, lam_re, lam_im, log_dt,
                                  b_re, b_im, c_re, c_im, d_s5, w_glu, b_glu)
    out = jnp.concatenate([y_a, y_b], axis=-1) @ w_out
    return x + out.astype(x.dtype), conv_new, ssd_new, s5_re, s5_im


def run_trunk(x, conv_s, ssd_s, s5re_s, s5im_s, layer_weights, final_norm_g):
    convs, ssds, res, ims = [], [], [], []
    for l in range(DEPTH):
        x, c, s, sr, si = mixer_layer(x, conv_s[l], ssd_s[l], s5re_s[l], s5im_s[l],
                                      *[w[l] for w in layer_weights])
        convs.append(c)
        ssds.append(s)
        res.append(sr)
        ims.append(si)
    return rmsnorm(x, final_norm_g), jnp.stack(convs), jnp.stack(ssds), jnp.stack(res), jnp.stack(ims)


def setup_inputs(seed: int = 0) -> dict:
    key = jax.random.key(seed)
    ks = jax.random.split(key, 26)
    f32 = jnp.float32

    def nrm(k, shape, scale):
        return scale * jax.random.normal(k, shape, f32)

    log_lo, log_hi = math.log(1e-3), math.log(1e-1)
    dt0 = jnp.exp(jax.random.uniform(ks[9], (DEPTH, SSD_HEADS), f32, log_lo, log_hi))
    return {
        'x_prompt': nrm(ks[0], (BATCH, SEQ, D_MODEL), 1.0),
        'x_sample': nrm(ks[1], (DEC_BATCH, DEC_SEQ, D_MODEL), 1.0),
        'state_ssd_conv': nrm(ks[2], (DEPTH, DEC_BATCH, SSD_CONV - 1, SSD_CONV_DIM), 1.0),
        'state_ssd': nrm(ks[3], (DEPTH, DEC_BATCH, SSD_HEADS, SSD_HEADDIM, SSD_STATE), 0.5),
        'state_s5_re': nrm(ks[4], (DEPTH, DEC_BATCH, S5_GROUPS, S5_STATE), 0.5),
        'state_s5_im': nrm(ks[5], (DEPTH, DEC_BATCH, S5_GROUPS, S5_STATE), 0.5),
        'norm_g': 1.0 + nrm(ks[6], (DEPTH, D_MODEL), 0.02),
        'w_in': nrm(ks[7], (DEPTH, D_MODEL, IN_COLS), D_MODEL ** -0.5),
        'conv_w': nrm(ks[8], (DEPTH, SSD_CONV, SSD_CONV_DIM), SSD_CONV ** -0.5),
        'conv_b': nrm(ks[10], (DEPTH, SSD_CONV_DIM), 0.02),
        'dt_bias': dt0 + jnp.log(-jnp.expm1(-dt0)),
        'a_log': jnp.log(jax.random.uniform(ks[11], (DEPTH, SSD_HEADS), f32, 1.0, 16.0)),
        'd_ssd': 1.0 + nrm(ks[12], (DEPTH, SSD_HEADS), 0.1),
        'ssd_norm_g': 1.0 + nrm(ks[13], (DEPTH, D_SSD), 0.02),
        'lam_re': -0.5 + nrm(ks[14], (DEPTH, S5_GROUPS, S5_STATE), 0.01),
        'lam_im': math.pi * jnp.arange(S5_STATE, dtype=f32) + nrm(ks[15], (DEPTH, S5_GROUPS, S5_STATE), 0.01),
        'log_dt': jax.random.uniform(ks[16], (DEPTH, S5_GROUPS), f32, log_lo, log_hi),
        'b_re': nrm(ks[17], (DEPTH, S5_GROUPS, S5_STATE, S5_GROUP_CH), (2 * S5_GROUP_CH) ** -0.5),
        'b_im': nrm(ks[18], (DEPTH, S5_GROUPS, S5_STATE, S5_GROUP_CH), (2 * S5_GROUP_CH) ** -0.5),
        'c_re': nrm(ks[19], (DEPTH, S5_GROUPS, S5_GROUP_CH, S5_STATE), S5_STATE ** -0.5),
        'c_im': nrm(ks[20], (DEPTH, S5_GROUPS, S5_GROUP_CH, S5_STATE), S5_STATE ** -0.5),
        'd_s5': nrm(ks[21], (DEPTH, D_S5), 1.0),
        'w_glu': nrm(ks[22], (DEPTH, D_S5, D_S5), D_S5 ** -0.5),
        'b_glu': nrm(ks[23], (DEPTH, D_S5), 0.02),
        'w_out': nrm(ks[24], (DEPTH, D_INNER, D_MODEL), D_INNER ** -0.5),
        'final_norm_g': 1.0 + nrm(ks[25], (D_MODEL,), 0.02),
    }


def reference(x_prompt, x_sample, state_ssd_conv, state_ssd, state_s5_re, state_s5_im,
              norm_g, w_in, conv_w, conv_b, dt_bias, a_log, d_ssd, ssd_norm_g,
              lam_re, lam_im, log_dt, b_re, b_im, c_re, c_im, d_s5, w_glu, b_glu, w_out,
              final_norm_g):
    layer_weights = (norm_g, w_in, conv_w, conv_b, dt_bias, a_log, d_ssd, ssd_norm_g,
                     lam_re, lam_im, log_dt, b_re, b_im, c_re, c_im, d_s5, w_glu, b_glu, w_out)
    bp = x_prompt.shape[0]
    zero_conv = jnp.zeros((DEPTH, bp, SSD_CONV - 1, SSD_CONV_DIM), x_prompt.dtype)
    zero_ssd = jnp.zeros((DEPTH, bp, SSD_HEADS, SSD_HEADDIM, SSD_STATE), jnp.float32)
    zero_s5 = jnp.zeros((DEPTH, bp, S5_GROUPS, S5_STATE), jnp.float32)
    y_prompt, conv_p, ssd_p, s5re_p, s5im_p = run_trunk(
        x_prompt, zero_conv, zero_ssd, zero_s5, zero_s5, layer_weights, final_norm_g)
    y_sample, conv_s, ssd_s, s5re_s, s5im_s = run_trunk(
        x_sample, state_ssd_conv, state_ssd, state_s5_re, state_s5_im, layer_weights, final_norm_g)
    return (y_prompt, y_sample, conv_p, ssd_p, s5re_p, s5im_p, conv_s, ssd_s, s5re_s, s5im_s)
```

```python
import functools
import math

import jax
import jax.numpy as jnp
from jax import lax
from jax.experimental import pallas as pl
from jax.experimental.pallas import tpu as pltpu

D_MODEL = 1024
DEPTH = 4
D_SSD = 1024
D_S5 = 1024
SSD_HEADDIM = 64
SSD_HEADS = 16
SSD_GROUPS = 4
SSD_STATE = 128
SSD_CONV = 4
SSD_CONV_DIM = D_SSD + 2 * SSD_GROUPS * SSD_STATE
S5_GROUP_CH = 16
S5_GROUPS = 64
S5_STATE = 64
S5_LANES = S5_GROUPS * S5_STATE
EPS = 1e-6

LANE = 128
DT_PAD = LANE
S5_KT = 256
S5_NT = 1024
S5_NJ = D_S5 // S5_KT
S5_SLAB = 512
VMEM_LIMIT = 56 * 1024 * 1024

F32 = jnp.float32
BF16 = jnp.bfloat16


def _sigmoid(x):
    return 1.0 / (1.0 + jnp.exp(-x))


def _silu(x):
    return x * _sigmoid(x)


def _softplus(x):
    return jnp.maximum(x, 0.0) + jnp.log1p(jnp.exp(-jnp.abs(x)))


def _gelu_tanh(x):
    c = math.sqrt(2.0 / math.pi)
    return 0.5 * x * (1.0 + jnp.tanh(c * (x + 0.044715 * (x * x * x))))


def _dot(a, b):
    return jnp.dot(a, b, preferred_element_type=F32)


def _inproj_kernel(x_ref, g_ref, wza_ref, wxbc_ref, wdt_ref, wu_ref, wzb_ref,
                   za_ref, xbc_ref, dt_ref, u_ref, zb_ref):
    x = x_ref[...]
    ms = jnp.mean(x * x, axis=-1, keepdims=True)
    h = ((x * lax.rsqrt(ms + EPS)) * g_ref[...]).astype(BF16)
    za_ref[...] = _dot(h, wza_ref[...])
    xbc_ref[...] = _dot(h, wxbc_ref[...])
    dt_ref[...] = _dot(h, wdt_ref[...])
    u_ref[...] = _dot(h, wu_ref[...])
    zb_ref[...] = _dot(h, wzb_ref[...])


def _inproj(x, g, wza, wxbc, wdt, wu, wzb, tm):
    bsz, length, _ = x.shape
    grid = (bsz, length // tm)
    full = lambda shape: pl.BlockSpec(shape, lambda b, i: (0,) * len(shape))
    col = lambda n: pl.BlockSpec((tm, n), lambda b, i: (i, b))
    outs = [D_SSD, SSD_CONV_DIM, DT_PAD, D_S5, D_S5]
    return pl.pallas_call(
        _inproj_kernel,
        grid=grid,
        in_specs=[pl.BlockSpec((None, tm, D_MODEL), lambda b, i: (b, i, 0)),
                  full((1, D_MODEL)), full(wza.shape), full(wxbc.shape), full(wdt.shape),
                  full(wu.shape), full(wzb.shape)],
        out_specs=[col(n) for n in outs],
        out_shape=[jax.ShapeDtypeStruct((length, bsz * n), F32) for n in outs],
        compiler_params=pltpu.CompilerParams(
            dimension_semantics=("parallel", "parallel"), vmem_limit_bytes=VMEM_LIMIT),
        name="inproj",
    )(x, g, wza, wxbc, wdt, wu, wzb)


def _cumsum_rows(x):
    n = x.shape[0]
    row = lax.broadcasted_iota(jnp.int32, x.shape, 0)
    s = 1
    while s < n:
        x = x + jnp.where(row >= s, pltpu.roll(x, s, axis=0), 0.0)
        s *= 2
    return x


def _ssd_kernel(*refs, q, has_state):
    if has_state:
        (za_ref, xbc_ref, dt_ref, convp_ref, h0_ref, cw_ref, cb_ref, dtb_ref, alog_ref,
         dexp_ref, ng_ref, expand_ref,
         y_ref, convo_ref, ho_ref, xp_ref, st_ref) = refs
    else:
        (za_ref, xbc_ref, dt_ref, cw_ref, cb_ref, dtb_ref, alog_ref,
         dexp_ref, ng_ref, expand_ref,
         y_ref, convo_ref, ho_ref, xp_ref, st_ref) = refs
    c = pl.program_id(1)
    last = pl.num_programs(1) - 1
    pad = 8
    k1 = SSD_CONV - 1

    @pl.when(c == 0)
    def _():
        xp_ref[0:pad, :] = jnp.zeros((pad, SSD_CONV_DIM), F32)
        if has_state:
            xp_ref[pad - k1:pad, :] = convp_ref[...]
            st_ref[...] = h0_ref[...]
        else:
            st_ref[...] = jnp.zeros(st_ref.shape, F32)

    xp_ref[pad:pad + q, :] = xbc_ref[...]
    conv = cb_ref[...]
    for k in range(SSD_CONV):
        conv = conv + cw_ref[k:k + 1, :] * xp_ref[pad - k1 + k:pad - k1 + k + q, :]
    tail = xp_ref[q + pad - k1:q + pad, :]
    xp_ref[pad - k1:pad, :] = tail

    @pl.when(c == last)
    def _():
        convo_ref[...] = tail

    act = _silu(conv)
    xs = act[:, :D_SSD]
    bm = act[:, D_SSD:D_SSD + SSD_GROUPS * SSD_STATE]
    cm = act[:, D_SSD + SSD_GROUPS * SSD_STATE:]

    dt = _softplus(dt_ref[...] + dtb_ref[...])
    a = dt * (-jnp.exp(alog_ref[...]))
    a_cs = _cumsum_rows(a)
    if q % LANE == 0:
        a_cs_t = a_cs.T
    else:
        a_cs_t = jnp.concatenate([a_cs, jnp.zeros((LANE - q, LANE), F32)], axis=0).T[:, :q]

    both = jnp.concatenate([dt, a_cs], axis=0)
    both_e = jnp.dot(both, expand_ref[...], precision=lax.Precision.HIGHEST,
                     preferred_element_type=F32)
    dt_e = both_e[:q]
    acs_e = both_e[q:]
    from_h = jnp.exp(acs_e)
    to_end = jnp.exp(acs_e[q - 1:q, :] - acs_e)
    xdt = xs * dt_e
    xend = (xdt * to_end).astype(BF16)
    xdt_b = xdt.astype(BF16)

    li = lax.broadcasted_iota(jnp.int32, (q, q), 0)
    si = lax.broadcasted_iota(jnp.int32, (q, q), 1)
    causal = li >= si
    hp = SSD_HEADS // SSD_GROUPS
    gw = hp * SSD_HEADDIM
    for g in range(SSD_GROUPS):
        bg = bm[:, g * SSD_STATE:(g + 1) * SSD_STATE].astype(BF16)
        cg = cm[:, g * SSD_STATE:(g + 1) * SSD_STATE].astype(BF16)
        cbm = lax.dot_general(cg, bg, (((1,), (1,)), ((), ())), preferred_element_type=F32)
        st_g = st_ref[:, g * gw:(g + 1) * gw]
        y_g = _dot(cg, st_g.astype(BF16)) * from_h[:, g * gw:(g + 1) * gw]
        parts = []
        for r in range(hp):
            h = g * hp + r
            seg = a_cs[:, h:h + 1] - a_cs_t[h:h + 1, :]
            decay = jnp.exp(jnp.where(causal, seg, -jnp.inf))
            m = (cbm * decay).astype(BF16)
            parts.append(_dot(m, xdt_b[:, h * SSD_HEADDIM:(h + 1) * SSD_HEADDIM]))
        y_g = y_g + jnp.concatenate(parts, axis=1)
        upd = lax.dot_general(bg, xend[:, g * gw:(g + 1) * gw], (((0,), (0,)), ((), ())),
                              preferred_element_type=F32)
        st_ref[:, g * gw:(g + 1) * gw] = st_g * from_h[q - 1:q, g * gw:(g + 1) * gw] + upd
        sl = slice(g * gw, (g + 1) * gw)
        y_g = y_g + dexp_ref[:, sl] * xs[:, sl]
        y_g = y_g * _silu(za_ref[:, sl])
        ms = jnp.mean(y_g * y_g, axis=-1, keepdims=True)
        y_ref[:, sl] = (y_g * lax.rsqrt(ms + EPS)) * ng_ref[:, sl]

    @pl.when(c == last)
    def _():
        ho_ref[...] = st_ref[...]


def _ssd(za, xbc, dt, conv_prev, h0, cw, cb, dtb, alog, dexp, ng, expand, bsz, q):
    length = za.shape[0]
    has_state = conv_prev is not None
    grid = (bsz, length // q)
    col = lambda n: pl.BlockSpec((q, n), lambda b, c: (c, b))
    full = lambda shape: pl.BlockSpec(shape, lambda b, c: (0,) * len(shape))
    per_b = lambda shape: pl.BlockSpec((None,) + shape, lambda b, c: (b, 0, 0))
    k1 = SSD_CONV - 1
    in_specs = [col(D_SSD), col(SSD_CONV_DIM), col(DT_PAD)]
    args = [za, xbc, dt]
    if has_state:
        in_specs += [per_b((k1, SSD_CONV_DIM)), per_b((SSD_STATE, D_SSD))]
        args += [conv_prev, h0]
    params = [cw, cb, dtb, alog, dexp, ng, expand]
    in_specs += [full(p.shape) for p in params]
    args += params
    return pl.pallas_call(
        functools.partial(_ssd_kernel, q=q, has_state=has_state),
        grid=grid,
        in_specs=in_specs,
        out_specs=[col(D_SSD), per_b((k1, SSD_CONV_DIM)), per_b((SSD_STATE, D_SSD))],
        out_shape=[jax.ShapeDtypeStruct((length, bsz * D_SSD), F32),
                   jax.ShapeDtypeStruct((bsz, k1, SSD_CONV_DIM), F32),
                   jax.ShapeDtypeStruct((bsz, SSD_STATE, D_SSD), F32)],
        scratch_shapes=[pltpu.VMEM((q + 8, SSD_CONV_DIM), F32),
                        pltpu.VMEM((SSD_STATE, D_SSD), F32)],
        compiler_params=pltpu.CompilerParams(
            dimension_semantics=("parallel", "arbitrary"), vmem_limit_bytes=VMEM_LIMIT),
        name="ssd",
    )(*args)


def _s5_kernel(*refs, bsz, tq, has_state):
    if has_state:
        (u_ref, zb_ref, h0r_ref, h0i_ref, lr_ref, li_ref, bre_ref, bim_ref, cre_ref, cim_ref,
         d_ref, wg_ref, bg_ref, y_ref, hro_ref, hio_ref, bur_ref, bui_ref, hr_ref, hi_ref) = refs
    else:
        (u_ref, zb_ref, lr_ref, li_ref, bre_ref, bim_ref, cre_ref, cim_ref,
         d_ref, wg_ref, bg_ref, y_ref, hro_ref, hio_ref, bur_ref, bui_ref, hr_ref, hi_ref) = refs
    i = pl.program_id(0)
    last = pl.num_programs(0) - 1

    @pl.when(i == 0)
    def _():
        if has_state:
            hr_ref[...] = h0r_ref[...]
            hi_ref[...] = h0i_ref[...]
        else:
            hr_ref[...] = jnp.zeros(hr_ref.shape, F32)
            hi_ref[...] = jnp.zeros(hi_ref.shape, F32)

    u = u_ref[...]
    u_b = u.astype(BF16)
    y_parts = []
    for j in range(S5_NJ):
        uj = u_b[:, j * S5_KT:(j + 1) * S5_KT]
        bur_ref[...] = _dot(uj, bre_ref[j])
        bui_ref[...] = _dot(uj, bim_ref[j])
        for s in range(S5_NT // S5_SLAB):
            lo = j * S5_NT + s * S5_SLAB
            cols = slice(s * S5_SLAB, (s + 1) * S5_SLAB)
            lam_r = jnp.broadcast_to(lr_ref[:, lo:lo + S5_SLAB], (bsz, S5_SLAB))
            lam_i = jnp.broadcast_to(li_ref[:, lo:lo + S5_SLAB], (bsz, S5_SLAB))

            def step(t, carry, cols=cols, lam_r=lam_r, lam_i=lam_i):
                hr, hi = carry
                rows = pl.ds(pl.multiple_of(t * bsz, bsz), bsz)
                nr = lam_r * hr - lam_i * hi + bur_ref[rows, cols]
                ni = lam_r * hi + lam_i * hr + bui_ref[rows, cols]
                bur_ref[rows, cols] = nr
                bui_ref[rows, cols] = ni
                return nr, ni

            hr, hi = lax.fori_loop(0, tq, step,
                                   (hr_ref[:, lo:lo + S5_SLAB], hi_ref[:, lo:lo + S5_SLAB]),
                                   unroll=4)
            hr_ref[:, lo:lo + S5_SLAB] = hr
            hi_ref[:, lo:lo + S5_SLAB] = hi
        y_parts.append(_dot(bur_ref[...].astype(BF16), cre_ref[j])
                       + _dot(bui_ref[...].astype(BF16), cim_ref[j]))
    y = jnp.concatenate(y_parts, axis=1) + d_ref[...] * u
    y = _gelu_tanh(y)
    y = y * _sigmoid(_dot(y.astype(BF16), wg_ref[...]) + bg_ref[...])
    y_ref[...] = y * _silu(zb_ref[...])

    @pl.when(i == last)
    def _():
        hro_ref[...] = hr_ref[...]
        hio_ref[...] = hi_ref[...]


def _s5(u, zb, h0r, h0i, lr, li, bre, bim, cre, cim, d, wg, bg, bsz, m):
    rows = u.shape[0]
    has_state = h0r is not None
    tq = m // bsz
    grid = (rows // m,)
    blk = pl.BlockSpec((m, D_S5), lambda i: (i, 0))
    full = lambda shape: pl.BlockSpec(shape, lambda i: (0,) * len(shape))
    in_specs = [blk, blk]
    args = [u, zb]
    if has_state:
        in_specs += [full(h0r.shape), full(h0i.shape)]
        args += [h0r, h0i]
    params = [lr, li, bre, bim, cre, cim, d, wg, bg]
    in_specs += [full(p.shape) for p in params]
    args += params
    st = jax.ShapeDtypeStruct((bsz, S5_LANES), F32)
    return pl.pallas_call(
        functools.partial(_s5_kernel, bsz=bsz, tq=tq, has_state=has_state),
        grid=grid,
        in_specs=in_specs,
        out_specs=[blk, full((bsz, S5_LANES)), full((bsz, S5_LANES))],
        out_shape=[jax.ShapeDtypeStruct((rows, D_S5), F32), st, st],
        scratch_shapes=[pltpu.VMEM((m, S5_NT), F32), pltpu.VMEM((m, S5_NT), F32),
                        pltpu.VMEM((bsz, S5_LANES), F32), pltpu.VMEM((bsz, S5_LANES), F32)],
        compiler_params=pltpu.CompilerParams(
            dimension_semantics=("arbitrary",), vmem_limit_bytes=VMEM_LIMIT),
        name="s5",
    )(*args)


def _outproj_kernel(*refs, final):
    if final:
        x_ref, ya_ref, yb_ref, w_ref, g_ref, o_ref = refs
    else:
        x_ref, ya_ref, yb_ref, w_ref, o_ref = refs
    y = jnp.concatenate([ya_ref[...], yb_ref[...]], axis=1).astype(BF16)
    x = x_ref[...] + _dot(y, w_ref[...])
    if final:
        ms = jnp.mean(x * x, axis=-1, keepdims=True)
        x = (x * lax.rsqrt(ms + EPS)) * g_ref[...]
    o_ref[...] = x


def _outproj(x, ya, yb, w, final_g, tm):
    bsz, length, _ = x.shape
    final = final_g is not None
    xblk = pl.BlockSpec((None, tm, D_MODEL), lambda b, i: (b, i, 0))
    col = pl.BlockSpec((tm, D_SSD), lambda b, i: (i, b))
    full = lambda shape: pl.BlockSpec(shape, lambda b, i: (0,) * len(shape))
    in_specs = [xblk, col, col, full(w.shape)]
    args = [x, ya, yb, w]
    if final:
        in_specs.append(full((1, D_MODEL)))
        args.append(final_g)
    return pl.pallas_call(
        functools.partial(_outproj_kernel, final=final),
        grid=(bsz, length // tm),
        in_specs=in_specs,
        out_specs=xblk,
        out_shape=jax.ShapeDtypeStruct(x.shape, F32),
        compiler_params=pltpu.CompilerParams(
            dimension_semantics=("parallel", "parallel"), vmem_limit_bytes=VMEM_LIMIT),
        name="outproj",
    )(*args)


def _prep_layer(l, norm_g, w_in, conv_w, conv_b, dt_bias, a_log, d_ssd, ssd_norm_g,
                lam_re, lam_im, log_dt, b_re, b_im, c_re, c_im, d_s5, w_glu, b_glu, w_out):
    w = w_in[l]
    o1 = D_SSD
    o2 = o1 + SSD_CONV_DIM
    o3 = o2 + SSD_HEADS
    o4 = o3 + D_S5
    wdt = jnp.zeros((D_MODEL, DT_PAD), F32).at[:, :SSD_HEADS].set(w[:, o2:o3])
    pad_h = lambda v: jnp.zeros((1, DT_PAD), F32).at[0, :SSD_HEADS].set(v)
    lam = lax.complex(lam_re[l], lam_im[l])
    lam_bar = jnp.exp(lam * jnp.exp(log_dt[l])[:, None])
    b_bar = ((lam_bar - 1.0) / lam)[..., None] * lax.complex(b_re[l], b_im[l])
    eye = jnp.eye(S5_KT // S5_GROUP_CH, dtype=F32)

    def blockdiag_b(v):
        v = v.reshape(S5_NJ, -1, S5_STATE, S5_GROUP_CH)
        return jnp.einsum('jgpc,gh->jgchp', v, eye).reshape(S5_NJ, S5_KT, S5_NT).astype(BF16)

    def blockdiag_c(v):
        v = v.reshape(S5_NJ, -1, S5_GROUP_CH, S5_STATE)
        return jnp.einsum('jgcp,gh->jgphc', v, eye).reshape(S5_NJ, S5_NT, S5_KT).astype(BF16)

    return dict(
        norm_g=norm_g[l][None, :],
        wza=w[:, :o1].astype(BF16), wxbc=w[:, o1:o2].astype(BF16), wdt=wdt.astype(BF16),
        wu=w[:, o3:o4].astype(BF16), wzb=w[:, o4:].astype(BF16),
        cw=conv_w[l], cb=conv_b[l][None, :], dtb=pad_h(dt_bias[l]),
        alog=pad_h(a_log[l]),
        dexp=jnp.repeat(d_ssd[l], SSD_HEADDIM)[None, :], ng=ssd_norm_g[l][None, :],
        lr=jnp.real(lam_bar).reshape(1, S5_LANES), li=jnp.imag(lam_bar).reshape(1, S5_LANES),
        bre=blockdiag_b(jnp.real(b_bar)), bim=blockdiag_b(jnp.imag(b_bar)),
        cre=blockdiag_c(c_re[l]), cim=blockdiag_c(-c_im[l]),
        d5=d_s5[l][None, :], wg=w_glu[l].astype(BF16), bg=b_glu[l][None, :],
        wout=w_out[l].astype(BF16),
    )


def _trunk(x, states, layers, final_g, expand, tm, q, m):
    bsz, length, _ = x.shape
    convs, ssds, res, ims = [], [], [], []
    for l, p in enumerate(layers):
        za, xbc, dt, u, zb = _inproj(x, p['norm_g'], p['wza'], p['wxbc'], p['wdt'], p['wu'],
                                     p['wzb'], tm)
        if states is None:
            conv_prev = h0 = h0r = h0i = None
        else:
            conv_s, ssd_s, s5re_s, s5im_s = states
            conv_prev = conv_s[l]
            h0 = jnp.transpose(ssd_s[l], (0, 3, 1, 2)).reshape(bsz, SSD_STATE, D_SSD)
            h0r = s5re_s[l].reshape(bsz, S5_LANES)
            h0i = s5im_s[l].reshape(bsz, S5_LANES)
        ya, conv_new, ssd_new = _ssd(za, xbc, dt, conv_prev, h0, p['cw'], p['cb'], p['dtb'],
                                     p['alog'], p['dexp'], p['ng'], expand, bsz, q)
        as_rows = lambda t: t.reshape(length * bsz, D_S5)
        yb, hr, hi = _s5(as_rows(u), as_rows(zb), h0r, h0i, p['lr'], p['li'], p['bre'], p['bim'],
                         p['cre'], p['cim'], p['d5'], p['wg'], p['bg'], bsz, m)
        yb = yb.reshape(length, bsz * D_S5)
        x = _outproj(x, ya, yb, p['wout'], final_g if l == DEPTH - 1 else None, tm)
        convs.append(conv_new)
        ssds.append(jnp.transpose(ssd_new.reshape(bsz, SSD_STATE, SSD_HEADS, SSD_HEADDIM),
                                  (0, 2, 3, 1)))
        res.append(hr.reshape(bsz, S5_GROUPS, S5_STATE))
        ims.append(hi.reshape(bsz, S5_GROUPS, S5_STATE))
    return x, jnp.stack(convs), jnp.stack(ssds), jnp.stack(res), jnp.stack(ims)


def kernel(x_prompt, x_sample, state_ssd_conv, state_ssd, state_s5_re, state_s5_im, norm_g, w_in, conv_w, conv_b, dt_bias, a_log, d_ssd, ssd_norm_g, lam_re, lam_im, log_dt, b_re, b_im, c_re, c_im, d_s5, w_glu, b_glu, w_out, final_norm_g):
    layers = [_prep_layer(l, norm_g, w_in, conv_w, conv_b, dt_bias, a_log, d_ssd, ssd_norm_g,
                          lam_re, lam_im, log_dt, b_re, b_im, c_re, c_im, d_s5, w_glu, b_glu,
                          w_out) for l in range(DEPTH)]
    final_g = final_norm_g[None, :]
    expand = (jnp.arange(DT_PAD)[:, None] == (jnp.arange(D_SSD) // SSD_HEADDIM)[None, :]).astype(F32)
    yp, conv_p, ssd_p, re_p, im_p = _trunk(x_prompt, None, layers, final_g, expand,
                                           tm=256, q=128, m=512)
    ys, conv_s, ssd_s, re_s, im_s = _trunk(
        x_sample, (state_ssd_conv, state_ssd, state_s5_re, state_s5_im), layers, final_g, expand,
        tm=x_sample.shape[1], q=x_sample.shape[1], m=x_sample.shape[0] * x_sample.shape[1])
    return (yp, ys, conv_p, ssd_p, re_p, im_p, conv_s, ssd_s, re_s, im_s)
```

```python
import functools
import math

import jax
import jax.numpy as jnp
from jax import lax
from jax.experimental import pallas as pl
from jax.experimental.pallas import tpu as pltpu

D_MODEL = 1024
DEPTH = 4
D_SSD = 1024
D_S5 = 1024
SSD_HEADDIM = 64
SSD_HEADS = 16
SSD_GROUPS = 4
SSD_STATE = 128
SSD_CONV = 4
SSD_CONV_DIM = D_SSD + 2 * SSD_GROUPS * SSD_STATE
S5_GROUP_CH = 16
S5_GROUPS = 64
S5_STATE = 64
S5_LANES = S5_GROUPS * S5_STATE
EPS = 1e-6

LANE = 128
SUBLANE = 8
MXU_DIM = 256
DT_PAD = LANE
S5_KT = MXU_DIM
S5_NT = S5_KT // S5_GROUP_CH * S5_STATE
S5_NJ = D_S5 // S5_KT
S5_SLAB = 512
VMEM_LIMIT = 56 * 1024 * 1024

F32 = jnp.float32
BF16 = jnp.bfloat16


def _tiles(bsz, length):
    tp = MXU_DIM // bsz
    tq = max(tp, min(length, 512 // bsz))
    q = min(length, LANE)
    return dict(tq=tq, tp=tp, q=q, m=bsz * tq)


def _sigmoid(x):
    return 1.0 / (1.0 + jnp.exp(-x))


def _silu(x):
    return x * _sigmoid(x)


def _softplus(x):
    return jnp.maximum(x, 0.0) + jnp.log1p(jnp.exp(-jnp.abs(x)))


def _gelu_tanh(x):
    c = math.sqrt(2.0 / math.pi)
    return 0.5 * x * (1.0 + jnp.tanh(c * (x + 0.044715 * (x * x * x))))


def _dot(a, b):
    return jnp.dot(a, b, preferred_element_type=F32)


def _rmsnorm(x, g):
    ms = jnp.mean(x * x, axis=-1, keepdims=True)
    return (x * lax.rsqrt(ms + EPS)) * g


def _inproj_kernel(x_ref, g_ref, p_ref, wza_ref, wxbc_ref, wdt_ref, wu_ref, wzb_ref,
                   za_ref, xbc_ref, dt_ref, u_ref, zb_ref, *, bsz, tq, tp):
    m = bsz * tq
    x = x_ref[...].reshape(m, D_MODEL)
    h = _rmsnorm(x, g_ref[...]).astype(BF16)
    za_ref[...] = _dot(h, wza_ref[...]).reshape(bsz, tq, D_SSD)
    xbc_ref[...] = _dot(h, wxbc_ref[...]).reshape(bsz, tq, SSD_CONV_DIM)
    dt_ref[...] = _dot(h, wdt_ref[...]).reshape(bsz, tq, DT_PAD)
    h3 = h.reshape(bsz, tq, D_MODEL)
    parts = [_dot(p_ref[...], h3[:, s * tp:(s + 1) * tp, :].reshape(bsz * tp, D_MODEL)).astype(BF16)
             for s in range(tq // tp)]
    h_t = jnp.concatenate(parts, axis=0)
    u_ref[...] = _dot(h_t, wu_ref[...])
    zb_ref[...] = _dot(h_t, wzb_ref[...])


def _inproj(x, g, perm, wza, wxbc, wdt, wu, wzb, tq, tp):
    bsz, length, _ = x.shape
    m = bsz * tq
    full = lambda shape: pl.BlockSpec(shape, lambda i: (0,) * len(shape))
    bmaj = lambda n: pl.BlockSpec((bsz, tq, n), lambda i: (0, i, 0))
    tmaj = pl.BlockSpec((m, D_S5), lambda i: (i, 0))
    return pl.pallas_call(
        functools.partial(_inproj_kernel, bsz=bsz, tq=tq, tp=tp),
        grid=(length // tq,),
        in_specs=[bmaj(D_MODEL), full((1, D_MODEL)), full(perm.shape), full(wza.shape),
                  full(wxbc.shape), full(wdt.shape), full(wu.shape), full(wzb.shape)],
        out_specs=[bmaj(D_SSD), bmaj(SSD_CONV_DIM), bmaj(DT_PAD), tmaj, tmaj],
        out_shape=[jax.ShapeDtypeStruct((bsz, length, n), F32)
                   for n in (D_SSD, SSD_CONV_DIM, DT_PAD)]
                  + [jax.ShapeDtypeStruct((length * bsz, D_S5), F32)] * 2,
        compiler_params=pltpu.CompilerParams(
            dimension_semantics=("parallel",), vmem_limit_bytes=VMEM_LIMIT),
        name="inproj",
    )(x, g, perm, wza, wxbc, wdt, wu, wzb)


def _cumsum_rows(x):
    n = x.shape[0]
    row = lax.broadcasted_iota(jnp.int32, x.shape, 0)
    s = 1
    while s < n:
        x = x + jnp.where(row >= s, pltpu.roll(x, s, axis=0), 0.0)
        s *= 2
    return x


def _split3(x):
    b0 = x.astype(BF16)
    r = x - b0.astype(F32)
    b1 = r.astype(BF16)
    b2 = (r - b1.astype(F32)).astype(BF16)
    return b0, b1, b2


def _ssd_kernel(*refs, q, has_state):
    if has_state:
        (za_ref, xbc_ref, dt_ref, convp_ref, h0_ref, cw_ref, cb_ref, dtb_ref, alog_ref,
         dexp_ref, ng_ref, expand_ref,
         y_ref, convo_ref, ho_ref, prev_ref, st_ref) = refs
    else:
        (za_ref, xbc_ref, dt_ref, cw_ref, cb_ref, dtb_ref, alog_ref,
         dexp_ref, ng_ref, expand_ref,
         y_ref, convo_ref, ho_ref, prev_ref, st_ref) = refs
    c = pl.program_id(1)
    last = pl.num_programs(1) - 1
    k1 = SSD_CONV - 1

    @pl.when(c == 0)
    def _():
        prev_ref[...] = jnp.zeros(prev_ref.shape, F32)
        if has_state:
            prev_ref[SUBLANE - k1:SUBLANE, :] = convp_ref[...]
            st_ref[...] = h0_ref[...]
        else:
            st_ref[...] = jnp.zeros(st_ref.shape, F32)

    x_cur = xbc_ref[...]
    xx = jnp.concatenate([prev_ref[...], x_cur], axis=0)
    conv = cb_ref[...] + cw_ref[k1:k1 + 1, :] * x_cur
    for s in range(1, SSD_CONV):
        conv = conv + cw_ref[k1 - s:k1 - s + 1, :] * pltpu.roll(xx, s, axis=0)[SUBLANE:]
    prev_ref[...] = x_cur[q - SUBLANE:]

    @pl.when(c == last)
    def _():
        convo_ref[...] = x_cur[q - k1:]

    act = _silu(conv)
    xs = act[:, :D_SSD]
    bm = act[:, D_SSD:D_SSD + SSD_GROUPS * SSD_STATE]
    cm = act[:, D_SSD + SSD_GROUPS * SSD_STATE:]

    dt = _softplus(dt_ref[...] + dtb_ref[...])
    a = dt * (-jnp.exp(alog_ref[...]))
    a_cs = _cumsum_rows(a)
    if q % LANE == 0:
        a_cs_t = a_cs.T
    else:
        a_cs_t = jnp.concatenate([a_cs, jnp.zeros((LANE - q, LANE), F32)], axis=0).T[:, :q]

    both = jnp.concatenate(_split3(jnp.concatenate([dt, a_cs], axis=0)), axis=1)
    both_e = _dot(both, expand_ref[...])
    dt_e = both_e[:q]
    acs_e = both_e[q:]
    from_h = jnp.exp(acs_e)
    to_end = jnp.exp(acs_e[q - 1:q, :] - acs_e)
    xdt = xs * dt_e
    xend = (xdt * to_end).astype(BF16)
    xdt_b = xdt.astype(BF16)

    li = lax.broadcasted_iota(jnp.int32, (q, q), 0)
    si = lax.broadcasted_iota(jnp.int32, (q, q), 1)
    causal = li >= si
    hp = SSD_HEADS // SSD_GROUPS
    gw = hp * SSD_HEADDIM
    for g in range(SSD_GROUPS):
        bg = bm[:, g * SSD_STATE:(g + 1) * SSD_STATE].astype(BF16)
        cg = cm[:, g * SSD_STATE:(g + 1) * SSD_STATE].astype(BF16)
        cbm = lax.dot_general(cg, bg, (((1,), (1,)), ((), ())), preferred_element_type=F32)
        st_g = st_ref[:, g * gw:(g + 1) * gw]
        y_g = _dot(cg, st_g.astype(BF16)) * from_h[:, g * gw:(g + 1) * gw]
        parts = []
        for r in range(hp):
            h = g * hp + r
            seg = a_cs[:, h:h + 1] - a_cs_t[h:h + 1, :]
            decay = jnp.exp(jnp.where(causal, seg, -jnp.inf))
            m = (cbm * decay).astype(BF16)
            parts.append(_dot(m, xdt_b[:, h * SSD_HEADDIM:(h + 1) * SSD_HEADDIM]))
        y_g = y_g + jnp.concatenate(parts, axis=1)
        upd = lax.dot_general(bg, xend[:, g * gw:(g + 1) * gw], (((0,), (0,)), ((), ())),
                              preferred_element_type=F32)
        st_ref[:, g * gw:(g + 1) * gw] = st_g * from_h[q - 1:q, g * gw:(g + 1) * gw] + upd
        sl = slice(g * gw, (g + 1) * gw)
        y_g = y_g + dexp_ref[:, sl] * xs[:, sl]
        y_g = y_g * _silu(za_ref[:, sl])
        y_ref[:, sl] = _rmsnorm(y_g, ng_ref[:, sl])

    @pl.when(c == last)
    def _():
        ho_ref[...] = st_ref[...]


def _ssd(za, xbc, dt, conv_prev, h0, cw, cb, dtb, alog, dexp, ng, expand, q):
    bsz, length, _ = za.shape
    has_state = conv_prev is not None
    blk = lambda n: pl.BlockSpec((None, q, n), lambda b, c: (b, c, 0))
    full = lambda shape: pl.BlockSpec(shape, lambda b, c: (0,) * len(shape))
    per_b = lambda shape: pl.BlockSpec((None,) + shape, lambda b, c: (b, 0, 0))
    k1 = SSD_CONV - 1
    in_specs = [blk(D_SSD), blk(SSD_CONV_DIM), blk(DT_PAD)]
    args = [za, xbc, dt]
    if has_state:
        in_specs += [per_b((k1, SSD_CONV_DIM)), per_b((SSD_STATE, D_SSD))]
        args += [conv_prev, h0]
    params = [cw, cb, dtb, alog, dexp, ng, expand]
    in_specs += [full(p.shape) for p in params]
    args += params
    return pl.pallas_call(
        functools.partial(_ssd_kernel, q=q, has_state=has_state),
        grid=(bsz, length // q),
        in_specs=in_specs,
        out_specs=[blk(D_SSD), per_b((k1, SSD_CONV_DIM)), per_b((SSD_STATE, D_SSD))],
        out_shape=[jax.ShapeDtypeStruct((bsz, length, D_SSD), F32),
                   jax.ShapeDtypeStruct((bsz, k1, SSD_CONV_DIM), F32),
                   jax.ShapeDtypeStruct((bsz, SSD_STATE, D_SSD), F32)],
        scratch_shapes=[pltpu.VMEM((SUBLANE, SSD_CONV_DIM), F32),
                        pltpu.VMEM((SSD_STATE, D_SSD), F32)],
        compiler_params=pltpu.CompilerParams(
            dimension_semantics=("parallel", "arbitrary"), vmem_limit_bytes=VMEM_LIMIT),
        name="ssd",
    )(*args)


def _s5_kernel(*refs, bsz, tq, has_state):
    if has_state:
        (u_ref, zb_ref, h0r_ref, h0i_ref, lr_ref, li_ref, bre_ref, bim_ref, cre_ref, cim_ref,
         d_ref, wg_ref, bg_ref, y_ref, hro_ref, hio_ref, bur_ref, bui_ref, hr_ref, hi_ref) = refs
    else:
        (u_ref, zb_ref, lr_ref, li_ref, bre_ref, bim_ref, cre_ref, cim_ref,
         d_ref, wg_ref, bg_ref, y_ref, hro_ref, hio_ref, bur_ref, bui_ref, hr_ref, hi_ref) = refs
    i = pl.program_id(0)
    last = pl.num_programs(0) - 1

    @pl.when(i == 0)
    def _():
        if has_state:
            hr_ref[...] = h0r_ref[...]
            hi_ref[...] = h0i_ref[...]
        else:
            hr_ref[...] = jnp.zeros(hr_ref.shape, F32)
            hi_ref[...] = jnp.zeros(hi_ref.shape, F32)

    u = u_ref[...]
    u_b = u.astype(BF16)
    n_slab = S5_NT // S5_SLAB
    units = [(j, s) for j in range(S5_NJ) for s in range(n_slab)]

    def project_in(k):
        j, s = units[k]
        uj = u_b[:, j * S5_KT:(j + 1) * S5_KT]
        cols = slice(s * S5_SLAB, (s + 1) * S5_SLAB)
        bur_ref[k % 2] = _dot(uj, bre_ref[j, :, cols])
        bui_ref[k % 2] = _dot(uj, bim_ref[j, :, cols])

    y_parts = [None] * S5_NJ
    project_in(0)
    for k, (j, s) in enumerate(units):
        if k + 1 < len(units):
            project_in(k + 1)
        p = k % 2
        lo = j * S5_NT + s * S5_SLAB
        lam_r = jnp.broadcast_to(lr_ref[:, lo:lo + S5_SLAB], (bsz, S5_SLAB))
        lam_i = jnp.broadcast_to(li_ref[:, lo:lo + S5_SLAB], (bsz, S5_SLAB))
        hr = hr_ref[:, lo:lo + S5_SLAB]
        hi = hi_ref[:, lo:lo + S5_SLAB]
        for t in range(tq):
            rows = slice(t * bsz, (t + 1) * bsz)
            nr = lam_r * hr - lam_i * hi + bur_ref[p, rows, :]
            ni = lam_r * hi + lam_i * hr + bui_ref[p, rows, :]
            bur_ref[p, rows, :] = nr
            bui_ref[p, rows, :] = ni
            hr, hi = nr, ni
        hr_ref[:, lo:lo + S5_SLAB] = hr
        hi_ref[:, lo:lo + S5_SLAB] = hi
        crow = slice(s * S5_SLAB, (s + 1) * S5_SLAB)
        yk = (_dot(bur_ref[p].astype(BF16), cre_ref[j, crow, :])
              + _dot(bui_ref[p].astype(BF16), cim_ref[j, crow, :]))
        y_parts[j] = yk if y_parts[j] is None else y_parts[j] + yk
    y = jnp.concatenate(y_parts, axis=1) + d_ref[...] * u
    y = _gelu_tanh(y)
    y = y * _sigmoid(_dot(y.astype(BF16), wg_ref[...]) + bg_ref[...])
    y_ref[...] = y * _silu(zb_ref[...])

    @pl.when(i == last)
    def _():
        hro_ref[...] = hr_ref[...]
        hio_ref[...] = hi_ref[...]


def _s5(u, zb, h0r, h0i, lr, li, bre, bim, cre, cim, d, wg, bg, bsz, m):
    rows = u.shape[0]
    has_state = h0r is not None
    tq = m // bsz
    blk = pl.BlockSpec((m, D_S5), lambda i: (i, 0))
    full = lambda shape: pl.BlockSpec(shape, lambda i: (0,) * len(shape))
    in_specs = [blk, blk]
    args = [u, zb]
    if has_state:
        in_specs += [full(h0r.shape), full(h0i.shape)]
        args += [h0r, h0i]
    params = [lr, li, bre, bim, cre, cim, d, wg, bg]
    in_specs += [full(p.shape) for p in params]
    args += params
    st = jax.ShapeDtypeStruct((bsz, S5_LANES), F32)
    return pl.pallas_call(
        functools.partial(_s5_kernel, bsz=bsz, tq=tq, has_state=has_state),
        grid=(rows // m,),
        in_specs=in_specs,
        out_specs=[blk, full((bsz, S5_LANES)), full((bsz, S5_LANES))],
        out_shape=[jax.ShapeDtypeStruct((rows, D_S5), F32), st, st],
        scratch_shapes=[pltpu.VMEM((2, m, S5_SLAB), F32), pltpu.VMEM((2, m, S5_SLAB), F32),
                        pltpu.VMEM((bsz, S5_LANES), F32), pltpu.VMEM((bsz, S5_LANES), F32)],
        compiler_params=pltpu.CompilerParams(
            dimension_semantics=("arbitrary",), vmem_limit_bytes=VMEM_LIMIT),
        name="s5",
    )(*args)


def _outproj_kernel(*refs, bsz, tq, tp, final):
    if final:
        x_ref, ya_ref, yb_ref, pt_ref, w_ref, g_ref, o_ref = refs
    else:
        x_ref, ya_ref, yb_ref, pt_ref, w_ref, o_ref = refs
    m = bsz * tq
    mp = bsz * tp
    yb = yb_ref[...].astype(BF16)
    parts = [_dot(pt_ref[...], yb[s * mp:(s + 1) * mp, :]).astype(BF16).reshape(bsz, tp, D_S5)
             for s in range(tq // tp)]
    yb_b = jnp.concatenate(parts, axis=1).reshape(m, D_S5)
    ya = ya_ref[...].reshape(m, D_SSD).astype(BF16)
    x = x_ref[...].reshape(m, D_MODEL) + _dot(jnp.concatenate([ya, yb_b], axis=1), w_ref[...])
    if final:
        x = _rmsnorm(x, g_ref[...])
    o_ref[...] = x.reshape(bsz, tq, D_MODEL)


def _outproj(x, ya, yb, perm_t, w, final_g, tq, tp):
    bsz, length, _ = x.shape
    final = final_g is not None
    bmaj = pl.BlockSpec((bsz, tq, D_MODEL), lambda i: (0, i, 0))
    tmaj = pl.BlockSpec((bsz * tq, D_S5), lambda i: (i, 0))
    full = lambda shape: pl.BlockSpec(shape, lambda i: (0,) * len(shape))
    in_specs = [bmaj, bmaj, tmaj, full(perm_t.shape), full(w.shape)]
    args = [x, ya, yb, perm_t, w]
    if final:
        in_specs.append(full((1, D_MODEL)))
        args.append(final_g)
    return pl.pallas_call(
        functools.partial(_outproj_kernel, bsz=bsz, tq=tq, tp=tp, final=final),
        grid=(length // tq,),
        in_specs=in_specs,
        out_specs=bmaj,
        out_shape=jax.ShapeDtypeStruct(x.shape, F32),
        compiler_params=pltpu.CompilerParams(
            dimension_semantics=("parallel",), vmem_limit_bytes=VMEM_LIMIT),
        name="outproj",
    )(*args)


def _prep_layer(l, norm_g, w_in, conv_w, conv_b, dt_bias, a_log, d_ssd, ssd_norm_g,
                lam_re, lam_im, log_dt, b_re, b_im, c_re, c_im, d_s5, w_glu, b_glu, w_out):
    w = w_in[l]
    o1 = D_SSD
    o2 = o1 + SSD_CONV_DIM
    o3 = o2 + SSD_HEADS
    o4 = o3 + D_S5
    wdt = jnp.zeros((D_MODEL, DT_PAD), F32).at[:, :SSD_HEADS].set(w[:, o2:o3])
    pad_h = lambda v: jnp.zeros((1, DT_PAD), F32).at[0, :SSD_HEADS].set(v)
    lam = lax.complex(lam_re[l], lam_im[l])
    lam_bar = jnp.exp(lam * jnp.exp(log_dt[l])[:, None])
    b_bar = ((lam_bar - 1.0) / lam)[..., None] * lax.complex(b_re[l], b_im[l])
    eye = jnp.eye(S5_KT // S5_GROUP_CH, dtype=F32)

    def blockdiag_b(v):
        v = v.reshape(S5_NJ, -1, S5_STATE, S5_GROUP_CH)
        return jnp.einsum('jgpc,gh->jgchp', v, eye).reshape(S5_NJ, S5_KT, S5_NT).astype(BF16)

    def blockdiag_c(v):
        v = v.reshape(S5_NJ, -1, S5_GROUP_CH, S5_STATE)
        return jnp.einsum('jgcp,gh->jgphc', v, eye).reshape(S5_NJ, S5_NT, S5_KT).astype(BF16)

    return dict(
        norm_g=norm_g[l][None, :],
        wza=w[:, :o1].astype(BF16), wxbc=w[:, o1:o2].astype(BF16), wdt=wdt.astype(BF16),
        wu=w[:, o3:o4].astype(BF16), wzb=w[:, o4:].astype(BF16),
        cw=conv_w[l], cb=conv_b[l][None, :], dtb=pad_h(dt_bias[l]), alog=pad_h(a_log[l]),
        dexp=jnp.repeat(d_ssd[l], SSD_HEADDIM)[None, :], ng=ssd_norm_g[l][None, :],
        lr=jnp.real(lam_bar).reshape(1, S5_LANES), li=jnp.imag(lam_bar).reshape(1, S5_LANES),
        bre=blockdiag_b(jnp.real(b_bar)), bim=blockdiag_b(jnp.imag(b_bar)),
        cre=blockdiag_c(c_re[l]), cim=blockdiag_c(-c_im[l]),
        d5=d_s5[l][None, :], wg=w_glu[l].astype(BF16), bg=b_glu[l][None, :],
        wout=w_out[l].astype(BF16),
    )


def _trunk(x, states, layers, final_g, expand):
    bsz, length, _ = x.shape
    tl = _tiles(bsz, length)
    tq, tp = tl['tq'], tl['tp']
    r = jnp.arange(bsz * tp)
    perm = (((r % bsz) * tp + r // bsz)[:, None] == r[None, :]).astype(BF16)
    convs, ssds, res, ims = [], [], [], []
    for l, p in enumerate(layers):
        za, xbc, dt, u, zb = _inproj(x, p['norm_g'], perm, p['wza'], p['wxbc'], p['wdt'],
                                     p['wu'], p['wzb'], tq, tp)
        if states is None:
            conv_prev = h0 = h0r = h0i = None
        else:
            conv_s, ssd_s, s5re_s, s5im_s = states
            conv_prev = conv_s[l]
            h0 = jnp.transpose(ssd_s[l], (0, 3, 1, 2)).reshape(bsz, SSD_STATE, D_SSD)
            h0r = s5re_s[l].reshape(bsz, S5_LANES)
            h0i = s5im_s[l].reshape(bsz, S5_LANES)
        ya, conv_new, ssd_new = _ssd(za, xbc, dt, conv_prev, h0, p['cw'], p['cb'], p['dtb'],
                                     p['alog'], p['dexp'], p['ng'], expand, tl['q'])
        yb, hr, hi = _s5(u, zb, h0r, h0i, p['lr'], p['li'], p['bre'], p['bim'],
                         p['cre'], p['cim'], p['d5'], p['wg'], p['bg'], bsz, tl['m'])
        x = _outproj(x, ya, yb, perm.T, p['wout'], final_g if l == DEPTH - 1 else None, tq, tp)
        convs.append(conv_new)
        ssds.append(jnp.transpose(ssd_new.reshape(bsz, SSD_STATE, SSD_HEADS, SSD_HEADDIM),
                                  (0, 2, 3, 1)))
        res.append(hr.reshape(bsz, S5_GROUPS, S5_STATE))
        ims.append(hi.reshape(bsz, S5_GROUPS, S5_STATE))
    return x, jnp.stack(convs), jnp.stack(ssds), jnp.stack(res), jnp.stack(ims)


def kernel(x_prompt, x_sample, state_ssd_conv, state_ssd, state_s5_re, state_s5_im, norm_g, w_in, conv_w, conv_b, dt_bias, a_log, d_ssd, ssd_norm_g, lam_re, lam_im, log_dt, b_re, b_im, c_re, c_im, d_s5, w_glu, b_glu, w_out, final_norm_g):
    layers = [_prep_layer(l, norm_g, w_in, conv_w, conv_b, dt_bias, a_log, d_ssd, ssd_norm_g,
                          lam_re, lam_im, log_dt, b_re, b_im, c_re, c_im, d_s5, w_glu, b_glu,
                          w_out) for l in range(DEPTH)]
    final_g = final_norm_g[None, :]
    e = (jnp.arange(DT_PAD)[:, None] == (jnp.arange(D_SSD) // SSD_HEADDIM)[None, :]).astype(BF16)
    expand = jnp.concatenate([e, e, e], axis=0)
    yp, conv_p, ssd_p, re_p, im_p = _trunk(x_prompt, None, layers, final_g, expand)
    ys, conv_s, ssd_s, re_s, im_s = _trunk(
        x_sample, (state_ssd_conv, state_ssd, state_s5_re, state_s5_im), layers, final_g, expand)
    return (yp, ys, conv_p, ssd_p, re_p, im_p, conv_s, ssd_s, re_s, im_s)
```

```python
import functools
import math

import jax
import jax.numpy as jnp
from jax import lax
from jax.experimental import pallas as pl
from jax.experimental.pallas import tpu as pltpu

D_MODEL = 1024
DEPTH = 4
D_SSD = 1024
D_S5 = 1024
SSD_HEADDIM = 64
SSD_HEADS = 16
SSD_GROUPS = 4
SSD_STATE = 128
SSD_CONV = 4
SSD_CONV_DIM = D_SSD + 2 * SSD_GROUPS * SSD_STATE
S5_GROUP_CH = 16
S5_GROUPS = 64
S5_STATE = 64
S5_LANES = S5_GROUPS * S5_STATE
EPS = 1e-6

LANE = 128
SUBLANE = 8
MXU_DIM = 256
DT_PAD = LANE
S5_KT = MXU_DIM
S5_NT = S5_KT // S5_GROUP_CH * S5_STATE
S5_NJ = D_S5 // S5_KT
S5_SLAB = 512
SSD_BG = 4
VMEM_LIMIT = 56 * 1024 * 1024

F32 = jnp.float32
BF16 = jnp.bfloat16


def _tiles(bsz, length):
    tp = MXU_DIM // bsz
    tq = max(tp, min(length, 512 // bsz))
    q = min(length, LANE)
    return dict(tq=tq, tp=tp, q=q, m=bsz * tq, bg=SSD_BG)


def _sigmoid(x):
    return 1.0 / (1.0 + jnp.exp(-x))


def _silu(x):
    return x * _sigmoid(x)


def _softplus(x):
    return jnp.maximum(x, 0.0) + jnp.log1p(jnp.exp(-jnp.abs(x)))


def _gelu_tanh(x):
    c = math.sqrt(2.0 / math.pi)
    return 0.5 * x * (1.0 + jnp.tanh(c * (x + 0.044715 * (x * x * x))))


def _dot(a, b):
    return jnp.dot(a, b, preferred_element_type=F32)


def _rmsnorm(x, g):
    ms = jnp.mean(x * x, axis=-1, keepdims=True)
    return (x * lax.rsqrt(ms + EPS)) * g


def _inproj_kernel(x_ref, g_ref, p_ref, wu_ref, wzb_ref, u_ref, zb_ref, *, bsz, tq, tp):
    m = bsz * tq
    x = x_ref[...].reshape(m, D_MODEL)
    h = _rmsnorm(x, g_ref[...]).astype(BF16)
    h3 = h.reshape(bsz, tq, D_MODEL)
    parts = [_dot(p_ref[...], h3[:, s * tp:(s + 1) * tp, :].reshape(bsz * tp, D_MODEL)).astype(BF16)
             for s in range(tq // tp)]
    h_t = jnp.concatenate(parts, axis=0)
    u_ref[...] = _dot(h_t, wu_ref[...])
    zb_ref[...] = _dot(h_t, wzb_ref[...])


def _inproj(x, g, perm, wu, wzb, tq, tp):
    bsz, length, _ = x.shape
    m = bsz * tq
    full = lambda shape: pl.BlockSpec(shape, lambda i: (0,) * len(shape))
    tmaj = pl.BlockSpec((m, D_S5), lambda i: (i, 0))
    return pl.pallas_call(
        functools.partial(_inproj_kernel, bsz=bsz, tq=tq, tp=tp),
        grid=(length // tq,),
        in_specs=[pl.BlockSpec((bsz, tq, D_MODEL), lambda i: (0, i, 0)), full((1, D_MODEL)),
                  full(perm.shape), full(wu.shape), full(wzb.shape)],
        out_specs=[tmaj, tmaj],
        out_shape=[jax.ShapeDtypeStruct((length * bsz, D_S5), F32)] * 2,
        compiler_params=pltpu.CompilerParams(
            dimension_semantics=("parallel",), vmem_limit_bytes=VMEM_LIMIT),
        name="inproj",
    )(x, g, perm, wu, wzb)


def _cumsum_rows(x):
    n = x.shape[0]
    row = lax.broadcasted_iota(jnp.int32, x.shape, 0)
    s = 1
    while s < n:
        x = x + jnp.where(row >= s, pltpu.roll(x, s, axis=0), 0.0)
        s *= 2
    return x


def _split3(x):
    b0 = x.astype(BF16)
    r = x - b0.astype(F32)
    b1 = r.astype(BF16)
    b2 = (r - b1.astype(F32)).astype(BF16)
    return b0, b1, b2


def _ssd_chunk(za_v, xbc_v, dt_v, prev_v, st_v, y_v, valid, prm, q, after_group):
    cw_ref, cb_ref, dtb_ref, alog_ref, dexp_ref, ng_ref, expand_ref = prm
    k1 = SSD_CONV - 1
    x_cur = xbc_v[...]
    prev = prev_v[...]
    xx = jnp.concatenate([prev, x_cur], axis=0)
    conv = cb_ref[...] + cw_ref[k1:k1 + 1, :] * x_cur
    for s in range(1, SSD_CONV):
        conv = conv + cw_ref[k1 - s:k1 - s + 1, :] * pltpu.roll(xx, s, axis=0)[SUBLANE:]
    prev_v[...] = jnp.where(valid, x_cur[q - SUBLANE:], prev)

    act = _silu(conv)
    xs = act[:, :D_SSD]
    bm = act[:, D_SSD:D_SSD + SSD_GROUPS * SSD_STATE]
    cm = act[:, D_SSD + SSD_GROUPS * SSD_STATE:]

    dt = _softplus(dt_v[...] + dtb_ref[...])
    a = dt * (-jnp.exp(alog_ref[...]))
    a_cs = _cumsum_rows(a)
    if q % LANE == 0:
        a_cs_t = a_cs.T
    else:
        a_cs_t = jnp.concatenate([a_cs, jnp.zeros((LANE - q, LANE), F32)], axis=0).T[:, :q]

    both = jnp.concatenate(_split3(jnp.concatenate([dt, a_cs], axis=0)), axis=1)
    both_e = _dot(both, expand_ref[...])
    dt_e = both_e[:q]
    acs_e = both_e[q:]
    from_h = jnp.exp(acs_e)
    to_end = jnp.exp(acs_e[q - 1:q, :] - acs_e)
    xdt = xs * dt_e
    xend = (xdt * to_end).astype(BF16)
    xdt_b = xdt.astype(BF16)

    li = lax.broadcasted_iota(jnp.int32, (q, q), 0)
    si = lax.broadcasted_iota(jnp.int32, (q, q), 1)
    causal = li >= si
    hp = SSD_HEADS // SSD_GROUPS
    gw = hp * SSD_HEADDIM
    for g in range(SSD_GROUPS):
        sl = slice(g * gw, (g + 1) * gw)
        bg = bm[:, g * SSD_STATE:(g + 1) * SSD_STATE].astype(BF16)
        cg = cm[:, g * SSD_STATE:(g + 1) * SSD_STATE].astype(BF16)
        cbm = lax.dot_general(cg, bg, (((1,), (1,)), ((), ())), preferred_element_type=F32)
        st_g = st_v[:, sl]
        y_g = _dot(cg, st_g.astype(BF16)) * from_h[:, sl]
        parts = []
        for r in range(hp):
            h = g * hp + r
            seg = a_cs[:, h:h + 1] - a_cs_t[h:h + 1, :]
            decay = jnp.exp(jnp.where(causal, seg, -jnp.inf))
            m = (cbm * decay).astype(BF16)
            parts.append(_dot(m, xdt_b[:, h * SSD_HEADDIM:(h + 1) * SSD_HEADDIM]))
        y_g = y_g + jnp.concatenate(parts, axis=1)
        upd = lax.dot_general(bg, xend[:, sl], (((0,), (0,)), ((), ())),
                              preferred_element_type=F32)
        st_v[:, sl] = jnp.where(valid, st_g * from_h[q - 1:q, sl] + upd, st_g)
        y_g = y_g + dexp_ref[:, sl] * xs[:, sl]
        y_g = y_g * _silu(za_v[:, sl])
        y_v[:, sl] = _rmsnorm(y_g, ng_ref[:, sl])
        after_group(g)


def _ssd_kernel(*refs, q, bg, has_state):
    if has_state:
        x_ref, g_ref, wza_ref, wxbc_ref, wdt_ref, convp_ref, h0_ref = refs[:7]
        refs = refs[7:]
    else:
        x_ref, g_ref, wza_ref, wxbc_ref, wdt_ref = refs[:5]
        refs = refs[5:]
    prm = refs[:7]
    y_ref, convo_ref, ho_ref, za_s, xbc_s, dt_s, prev_ref, st_ref = refs[7:]
    c = pl.program_id(1)
    last = pl.num_programs(1) - 1
    k1 = SSD_CONV - 1

    @pl.when(c == 0)
    def _():
        prev_ref[...] = jnp.zeros(prev_ref.shape, F32)
        if has_state:
            prev_ref[:, SUBLANE - k1:, :] = convp_ref[...]
            st_ref[...] = h0_ref[...]
        else:
            st_ref[...] = jnp.zeros(st_ref.shape, F32)
        za_s[1] = jnp.zeros(za_s.shape[1:], F32)
        xbc_s[1] = jnp.zeros(xbc_s.shape[1:], F32)
        dt_s[1] = jnp.zeros(dt_s.shape[1:], F32)

    valid = c >= 1

    def step(slot):
        x = x_ref[...].reshape(bg * q, D_MODEL)
        h = _rmsnorm(x, g_ref[...]).astype(BF16)
        pieces = [(dst, w_ref, c0, min(MXU_DIM, w_ref.shape[1] - c0))
                  for dst, w_ref in ((xbc_s, wxbc_ref), (za_s, wza_ref), (dt_s, wdt_ref))
                  for c0 in range(0, w_ref.shape[1], MXU_DIM)]

        def project(k):
            dst, w_ref, c0, w = pieces[k]
            dst[slot, :, :, c0:c0 + w] = _dot(h, w_ref[:, c0:c0 + w]).reshape(bg, q, w)

        slots = bg * SSD_GROUPS
        per_slot = -(-len(pieces) // slots)
        o = 1 - slot
        for b in range(bg):
            def after_group(g, b=b):
                k0 = (b * SSD_GROUPS + g) * per_slot
                for k in range(k0, min(k0 + per_slot, len(pieces))):
                    project(k)
            _ssd_chunk(za_s.at[o, b], xbc_s.at[o, b], dt_s.at[o, b], prev_ref.at[b],
                       st_ref.at[b], y_ref.at[b], valid, prm, q, after_group)

    @pl.when(c % 2 == 0)
    def _():
        step(0)

    @pl.when(c % 2 == 1)
    def _():
        step(1)

    @pl.when(c == last)
    def _():
        convo_ref[...] = prev_ref[:, SUBLANE - k1:, :]
        ho_ref[...] = st_ref[...]


def _ssd(x, g, wza, wxbc, wdt, conv_prev, h0, cw, cb, dtb, alog, dexp, ng, expand, q, bg):
    bsz, length, _ = x.shape
    nc = length // q
    has_state = conv_prev is not None
    full = lambda shape: pl.BlockSpec(shape, lambda b, c: (0,) * len(shape))
    per_b = lambda shape: pl.BlockSpec((bg,) + shape, lambda b, c: (b, 0, 0))
    k1 = SSD_CONV - 1
    in_specs = [pl.BlockSpec((bg, q, D_MODEL), lambda b, c: (b, jnp.minimum(c, nc - 1), 0)),
                full((1, D_MODEL)), full(wza.shape), full(wxbc.shape), full(wdt.shape)]
    args = [x, g, wza, wxbc, wdt]
    if has_state:
        in_specs += [per_b((k1, SSD_CONV_DIM)), per_b((SSD_STATE, D_SSD))]
        args += [conv_prev, h0]
    params = [cw, cb, dtb, alog, dexp, ng, expand]
    in_specs += [full(p.shape) for p in params]
    args += params
    return pl.pallas_call(
        functools.partial(_ssd_kernel, q=q, bg=bg, has_state=has_state),
        grid=(bsz // bg, nc + 1),
        in_specs=in_specs,
        out_specs=[pl.BlockSpec((bg, q, D_SSD), lambda b, c: (b, jnp.maximum(c - 1, 0), 0)),
                   per_b((k1, SSD_CONV_DIM)), per_b((SSD_STATE, D_SSD))],
        out_shape=[jax.ShapeDtypeStruct((bsz, length, D_SSD), F32),
                   jax.ShapeDtypeStruct((bsz, k1, SSD_CONV_DIM), F32),
                   jax.ShapeDtypeStruct((bsz, SSD_STATE, D_SSD), F32)],
        scratch_shapes=[pltpu.VMEM((2, bg, q, D_SSD), F32),
                        pltpu.VMEM((2, bg, q, SSD_CONV_DIM), F32),
                        pltpu.VMEM((2, bg, q, DT_PAD), F32),
                        pltpu.VMEM((bg, SUBLANE, SSD_CONV_DIM), F32),
                        pltpu.VMEM((bg, SSD_STATE, D_SSD), F32)],
        compiler_params=pltpu.CompilerParams(
            dimension_semantics=("parallel", "arbitrary"), vmem_limit_bytes=VMEM_LIMIT),
        name="ssd",
    )(*args)


def _s5_kernel(*refs, bsz, tq, has_state):
    if has_state:
        (u_ref, zb_ref, h0r_ref, h0i_ref, lr_ref, li_ref, bre_ref, bim_ref, cre_ref, cim_ref,
         d_ref, wg_ref, bg_ref, y_ref, hro_ref, hio_ref, bur_ref, bui_ref, hr_ref, hi_ref) = refs
    else:
        (u_ref, zb_ref, lr_ref, li_ref, bre_ref, bim_ref, cre_ref, cim_ref,
         d_ref, wg_ref, bg_ref, y_ref, hro_ref, hio_ref, bur_ref, bui_ref, hr_ref, hi_ref) = refs
    i = pl.program_id(0)
    last = pl.num_programs(0) - 1

    @pl.when(i == 0)
    def _():
        if has_state:
            hr_ref[...] = h0r_ref[...]
            hi_ref[...] = h0i_ref[...]
        else:
            hr_ref[...] = jnp.zeros(hr_ref.shape, F32)
            hi_ref[...] = jnp.zeros(hi_ref.shape, F32)

    u = u_ref[...]
    u_b = u.astype(BF16)
    n_slab = S5_NT // S5_SLAB
    units = [(j, s) for j in range(S5_NJ) for s in range(n_slab)]

    def project_in(k):
        j, s = units[k]
        uj = u_b[:, j * S5_KT:(j + 1) * S5_KT]
        cols = slice(s * S5_SLAB, (s + 1) * S5_SLAB)
        bur_ref[k % 2] = _dot(uj, bre_ref[j, :, cols])
        bui_ref[k % 2] = _dot(uj, bim_ref[j, :, cols])

    y_parts = [None] * S5_NJ
    gate_parts = [None] * S5_NJ
    project_in(0)
    for k, (j, s) in enumerate(units):
        if k + 1 < len(units):
            project_in(k + 1)
        p = k % 2
        lo = j * S5_NT + s * S5_SLAB
        lam_r = jnp.broadcast_to(lr_ref[:, lo:lo + S5_SLAB], (bsz, S5_SLAB))
        lam_i = jnp.broadcast_to(li_ref[:, lo:lo + S5_SLAB], (bsz, S5_SLAB))
        hr = hr_ref[:, lo:lo + S5_SLAB]
        hi = hi_ref[:, lo:lo + S5_SLAB]
        for t in range(tq):
            rows = slice(t * bsz, (t + 1) * bsz)
            nr = lam_r * hr - lam_i * hi + bur_ref[p, rows, :]
            ni = lam_r * hi + lam_i * hr + bui_ref[p, rows, :]
            bur_ref[p, rows, :] = nr
            bui_ref[p, rows, :] = ni
            hr, hi = nr, ni
        hr_ref[:, lo:lo + S5_SLAB] = hr
        hi_ref[:, lo:lo + S5_SLAB] = hi
        crow = slice(s * S5_SLAB, (s + 1) * S5_SLAB)
        yk = (_dot(bur_ref[p].astype(BF16), cre_ref[j, crow, :])
              + _dot(bui_ref[p].astype(BF16), cim_ref[j, crow, :]))
        y_parts[j] = yk if y_parts[j] is None else y_parts[j] + yk
        if s == n_slab - 1:
            jc = slice(j * S5_KT, (j + 1) * S5_KT)
            y_parts[j] = _gelu_tanh(y_parts[j] + d_ref[:, jc] * u[:, jc])
            gate_parts[j] = y_parts[j] * _silu(zb_ref[:, jc])
    y_b = jnp.concatenate(y_parts, axis=1).astype(BF16)
    for n in range(S5_NJ):
        nc = slice(n * S5_KT, (n + 1) * S5_KT)
        z = _dot(y_b, wg_ref[:, nc]) + bg_ref[:, nc]
        y_ref[:, nc] = gate_parts[n] * _sigmoid(z)

    @pl.when(i == last)
    def _():
        hro_ref[...] = hr_ref[...]
        hio_ref[...] = hi_ref[...]


def _s5(u, zb, h0r, h0i, lr, li, bre, bim, cre, cim, d, wg, bg, bsz, m):
    rows = u.shape[0]
    has_state = h0r is not None
    tq = m // bsz
    blk = pl.BlockSpec((m, D_S5), lambda i: (i, 0))
    full = lambda shape: pl.BlockSpec(shape, lambda i: (0,) * len(shape))
    in_specs = [blk, blk]
    args = [u, zb]
    if has_state:
        in_specs += [full(h0r.shape), full(h0i.shape)]
        args += [h0r, h0i]
    params = [lr, li, bre, bim, cre, cim, d, wg, bg]
    in_specs += [full(p.shape) for p in params]
    args += params
    st = jax.ShapeDtypeStruct((bsz, S5_LANES), F32)
    return pl.pallas_call(
        functools.partial(_s5_kernel, bsz=bsz, tq=tq, has_state=has_state),
        grid=(rows // m,),
        in_specs=in_specs,
        out_specs=[blk, full((bsz, S5_LANES)), full((bsz, S5_LANES))],
        out_shape=[jax.ShapeDtypeStruct((rows, D_S5), F32), st, st],
        scratch_shapes=[pltpu.VMEM((2, m, S5_SLAB), F32), pltpu.VMEM((2, m, S5_SLAB), F32),
                        pltpu.VMEM((bsz, S5_LANES), F32), pltpu.VMEM((bsz, S5_LANES), F32)],
        compiler_params=pltpu.CompilerParams(
            dimension_semantics=("arbitrary",), vmem_limit_bytes=VMEM_LIMIT),
        name="s5",
    )(*args)


def _outproj_kernel(*refs, bsz, tq, tp, final):
    if final:
        x_ref, ya_ref, yb_ref, pt_ref, w_ref, g_ref, o_ref = refs
    else:
        x_ref, ya_ref, yb_ref, pt_ref, w_ref, o_ref = refs
    m = bsz * tq
    mp = bsz * tp
    yb = yb_ref[...].astype(BF16)
    parts = [_dot(pt_ref[...], yb[s * mp:(s + 1) * mp, :]).astype(BF16).reshape(bsz, tp, D_S5)
             for s in range(tq // tp)]
    yb_b = jnp.concatenate(parts, axis=1).reshape(m, D_S5)
    ya = ya_ref[...].reshape(m, D_SSD).astype(BF16)
    x = x_ref[...].reshape(m, D_MODEL) + _dot(jnp.concatenate([ya, yb_b], axis=1), w_ref[...])
    if final:
        x = _rmsnorm(x, g_ref[...])
    o_ref[...] = x.reshape(bsz, tq, D_MODEL)


def _outproj(x, ya, yb, perm_t, w, final_g, tq, tp):
    bsz, length, _ = x.shape
    final = final_g is not None
    bmaj = pl.BlockSpec((bsz, tq, D_MODEL), lambda i: (0, i, 0))
    tmaj = pl.BlockSpec((bsz * tq, D_S5), lambda i: (i, 0))
    full = lambda shape: pl.BlockSpec(shape, lambda i: (0,) * len(shape))
    in_specs = [bmaj, bmaj, tmaj, full(perm_t.shape), full(w.shape)]
    args = [x, ya, yb, perm_t, w]
    if final:
        in_specs.append(full((1, D_MODEL)))
        args.append(final_g)
    return pl.pallas_call(
        functools.partial(_outproj_kernel, bsz=bsz, tq=tq, tp=tp, final=final),
        grid=(length // tq,),
        in_specs=in_specs,
        out_specs=bmaj,
        out_shape=jax.ShapeDtypeStruct(x.shape, F32),
        compiler_params=pltpu.CompilerParams(
            dimension_semantics=("parallel",), vmem_limit_bytes=VMEM_LIMIT),
        name="outproj",
    )(*args)


def _prep_layer(l, norm_g, w_in, conv_w, conv_b, dt_bias, a_log, d_ssd, ssd_norm_g,
                lam_re, lam_im, log_dt, b_re, b_im, c_re, c_im, d_s5, w_glu, b_glu, w_out):
    w = w_in[l]
    o1 = D_SSD
    o2 = o1 + SSD_CONV_DIM
    o3 = o2 + SSD_HEADS
    o4 = o3 + D_S5
    wdt = jnp.zeros((D_MODEL, DT_PAD), F32).at[:, :SSD_HEADS].set(w[:, o2:o3])
    pad_h = lambda v: jnp.zeros((1, DT_PAD), F32).at[0, :SSD_HEADS].set(v)
    lam = lax.complex(lam_re[l], lam_im[l])
    lam_bar = jnp.exp(lam * jnp.exp(log_dt[l])[:, None])
    b_bar = ((lam_bar - 1.0) / lam)[..., None] * lax.complex(b_re[l], b_im[l])
    eye = jnp.eye(S5_KT // S5_GROUP_CH, dtype=F32)

    def blockdiag_b(v):
        v = v.reshape(S5_NJ, -1, S5_STATE, S5_GROUP_CH)
        return jnp.einsum('jgpc,gh->jgchp', v, eye).reshape(S5_NJ, S5_KT, S5_NT).astype(BF16)

    def blockdiag_c(v):
        v = v.reshape(S5_NJ, -1, S5_GROUP_CH, S5_STATE)
        return jnp.einsum('jgcp,gh->jgphc', v, eye).reshape(S5_NJ, S5_NT, S5_KT).astype(BF16)

    return dict(
        norm_g=norm_g[l][None, :],
        wza=w[:, :o1].astype(BF16), wxbc=w[:, o1:o2].astype(BF16), wdt=wdt.astype(BF16),
        wu=w[:, o3:o4].astype(BF16), wzb=w[:, o4:].astype(BF16),
        cw=conv_w[l], cb=conv_b[l][None, :], dtb=pad_h(dt_bias[l]), alog=pad_h(a_log[l]),
        dexp=jnp.repeat(d_ssd[l], SSD_HEADDIM)[None, :], ng=ssd_norm_g[l][None, :],
        lr=jnp.real(lam_bar).reshape(1, S5_LANES), li=jnp.imag(lam_bar).reshape(1, S5_LANES),
        bre=blockdiag_b(jnp.real(b_bar)), bim=blockdiag_b(jnp.imag(b_bar)),
        cre=blockdiag_c(c_re[l]), cim=blockdiag_c(-c_im[l]),
        d5=d_s5[l][None, :], wg=w_glu[l].astype(BF16), bg=b_glu[l][None, :],
        wout=w_out[l].astype(BF16),
    )


def _trunk(x, states, layers, final_g, expand):
    bsz, length, _ = x.shape
    tl = _tiles(bsz, length)
    tq, tp = tl['tq'], tl['tp']
    r = jnp.arange(bsz * tp)
    perm = (((r % bsz) * tp + r // bsz)[:, None] == r[None, :]).astype(BF16)
    convs, ssds, res, ims = [], [], [], []
    for l, p in enumerate(layers):
        u, zb = _inproj(x, p['norm_g'], perm, p['wu'], p['wzb'], tq, tp)
        if states is None:
            conv_prev = h0 = h0r = h0i = None
        else:
            conv_s, ssd_s, s5re_s, s5im_s = states
            conv_prev = conv_s[l]
            h0 = jnp.transpose(ssd_s[l], (0, 3, 1, 2)).reshape(bsz, SSD_STATE, D_SSD)
            h0r = s5re_s[l].reshape(bsz, S5_LANES)
            h0i = s5im_s[l].reshape(bsz, S5_LANES)
        ya, conv_new, ssd_new = _ssd(x, p['norm_g'], p['wza'], p['wxbc'], p['wdt'], conv_prev, h0,
                                     p['cw'], p['cb'], p['dtb'], p['alog'], p['dexp'], p['ng'],
                                     expand, tl['q'], tl['bg'])
        yb, hr, hi = _s5(u, zb, h0r, h0i, p['lr'], p['li'], p['bre'], p['bim'],
                         p['cre'], p['cim'], p['d5'], p['wg'], p['bg'], bsz, tl['m'])
        x = _outproj(x, ya, yb, perm.T, p['wout'], final_g if l == DEPTH - 1 else None, tq, tp)
        convs.append(conv_new)
        ssds.append(jnp.transpose(ssd_new.reshape(bsz, SSD_STATE, SSD_HEADS, SSD_HEADDIM),
                                  (0, 2, 3, 1)))
        res.append(hr.reshape(bsz, S5_GROUPS, S5_STATE))
        ims.append(hi.reshape(bsz, S5_GROUPS, S5_STATE))
    return x, jnp.stack(convs), jnp.stack(ssds), jnp.stack(res), jnp.stack(ims)


def kernel(x_prompt, x_sample, state_ssd_conv, state_ssd, state_s5_re, state_s5_im, norm_g, w_in, conv_w, conv_b, dt_bias, a_log, d_ssd, ssd_norm_g, lam_re, lam_im, log_dt, b_re, b_im, c_re, c_im, d_s5, w_glu, b_glu, w_out, final_norm_g):
    layers = [_prep_layer(l, norm_g, w_in, conv_w, conv_b, dt_bias, a_log, d_ssd, ssd_norm_g,
                          lam_re, lam_im, log_dt, b_re, b_im, c_re, c_im, d_s5, w_glu, b_glu,
                          w_out) for l in range(DEPTH)]
    final_g = final_norm_g[None, :]
    e = (jnp.arange(DT_PAD)[:, None] == (jnp.arange(D_SSD) // SSD_HEADDIM)[None, :]).astype(BF16)
    expand = jnp.concatenate([e, e, e], axis=0)
    yp, conv_p, ssd_p, re_p, im_p = _trunk(x_prompt, None, layers, final_g, expand)
    ys, conv_s, ssd_s, re_s, im_s = _trunk(
        x_sample, (state_ssd_conv, state_ssd, state_s5_re, state_s5_im), layers, final_g, expand)
    return (yp, ys, conv_p, ssd_p, re_p, im_p, conv_s, ssd_s, re_s, im_s)
```

```python
import functools
import math

import jax
import jax.numpy as jnp
from jax import lax
from jax.experimental import pallas as pl
from jax.experimental.pallas import tpu as pltpu

D_MODEL = 1024
DEPTH = 4
D_SSD = 1024
D_S5 = 1024
SSD_HEADDIM = 64
SSD_HEADS = 16
SSD_GROUPS = 4
SSD_STATE = 128
SSD_CONV = 4
SSD_CONV_DIM = D_SSD + 2 * SSD_GROUPS * SSD_STATE
S5_GROUP_CH = 16
S5_GROUPS = 64
S5_STATE = 64
S5_LANES = S5_GROUPS * S5_STATE
EPS = 1e-6

LANE = 128
SUBLANE = 8
MXU_DIM = 256
DT_PAD = LANE
S5_KT = MXU_DIM
S5_NT = S5_KT // S5_GROUP_CH * S5_STATE
S5_NJ = D_S5 // S5_KT
S5_SLAB = 512
SSD_BG = 4
VMEM_LIMIT = 56 * 1024 * 1024

F32 = jnp.float32
BF16 = jnp.bfloat16


def _tiles(bsz, length):
    tp = MXU_DIM // bsz
    tq = max(tp, min(length, 512 // bsz))
    q = min(length, LANE)
    return dict(tq=tq, tp=tp, q=q, m=bsz * tq, bg=SSD_BG)


LOG2E = math.log2(math.e)


def _sigmoid(x):
    return 1.0 / (1.0 + jnp.exp2(x * -LOG2E))


def _silu(x):
    return x * _sigmoid(x)


def _softplus(x):
    return jnp.maximum(x, 0.0) + jnp.log1p(jnp.exp(-jnp.abs(x)))


def _gelu_tanh(x):
    c = math.sqrt(2.0 / math.pi)
    return 0.5 * x * (1.0 + jnp.tanh(c * (x + 0.044715 * (x * x * x))))


def _dot(a, b):
    return jnp.dot(a, b, preferred_element_type=F32)


def _rmsnorm(x, g):
    ms = jnp.mean(x * x, axis=-1, keepdims=True)
    return (x * lax.rsqrt(ms + EPS)) * g


def _inproj_kernel(x_ref, g_ref, p_ref, wu_ref, wzb_ref, h_ref, u_ref, zb_ref, *, bsz, tq, tp):
    m = bsz * tq
    x = x_ref[...].reshape(m, D_MODEL)
    h = _rmsnorm(x, g_ref[...]).astype(BF16)
    h3 = h.reshape(bsz, tq, D_MODEL)
    h_ref[...] = h3
    parts = [_dot(p_ref[...], h3[:, s * tp:(s + 1) * tp, :].reshape(bsz * tp, D_MODEL)).astype(BF16)
             for s in range(tq // tp)]
    h_t = jnp.concatenate(parts, axis=0)
    u_ref[...] = _dot(h_t, wu_ref[...])
    zb_ref[...] = _dot(h_t, wzb_ref[...])


def _inproj(x, g, perm, wu, wzb, tq, tp):
    bsz, length, _ = x.shape
    m = bsz * tq
    full = lambda shape: pl.BlockSpec(shape, lambda i: (0,) * len(shape))
    tmaj = pl.BlockSpec((m, D_S5), lambda i: (i, 0))
    bmaj = pl.BlockSpec((bsz, tq, D_MODEL), lambda i: (0, i, 0))
    return pl.pallas_call(
        functools.partial(_inproj_kernel, bsz=bsz, tq=tq, tp=tp),
        grid=(length // tq,),
        in_specs=[bmaj, full((1, D_MODEL)), full(perm.shape), full(wu.shape), full(wzb.shape)],
        out_specs=[bmaj, tmaj, tmaj],
        out_shape=[jax.ShapeDtypeStruct(x.shape, BF16)]
                  + [jax.ShapeDtypeStruct((length * bsz, D_S5), F32)] * 2,
        compiler_params=pltpu.CompilerParams(
            dimension_semantics=("parallel",), vmem_limit_bytes=VMEM_LIMIT),
        name="inproj",
    )(x, g, perm, wu, wzb)


def _cumsum_rows(x):
    n = x.shape[0]
    row = lax.broadcasted_iota(jnp.int32, x.shape, 0)
    s = 1
    while s < n:
        x = x + jnp.where(row >= s, pltpu.roll(x, s, axis=0), 0.0)
        s *= 2
    return x


def _split3(x):
    b0 = x.astype(BF16)
    r = x - b0.astype(F32)
    b1 = r.astype(BF16)
    b2 = (r - b1.astype(F32)).astype(BF16)
    return b0, b1, b2


def _ssd_chunk(za_v, xbc_v, dt_v, prev_v, st_v, y_v, prm, q, after_group):
    cw_ref, cb_ref, dtb_ref, alog_ref, dexp_ref, ng_ref, expand_ref = prm
    k1 = SSD_CONV - 1
    x_cur = xbc_v[...]
    xx = jnp.concatenate([prev_v[...], x_cur], axis=0)
    conv = cb_ref[...] + cw_ref[k1:k1 + 1, :] * x_cur
    for s in range(1, SSD_CONV):
        conv = conv + cw_ref[k1 - s:k1 - s + 1, :] * pltpu.roll(xx, s, axis=0)[SUBLANE:]
    prev_v[...] = x_cur[q - SUBLANE:]

    act = _silu(conv)
    xs = act[:, :D_SSD]
    bm = act[:, D_SSD:D_SSD + SSD_GROUPS * SSD_STATE]
    cm = act[:, D_SSD + SSD_GROUPS * SSD_STATE:]

    dt = _softplus(dt_v[...] + dtb_ref[...])
    a = dt * (-jnp.exp(alog_ref[...]))
    a_cs = _cumsum_rows(a) * LOG2E
    if q % LANE == 0:
        a_cs_t = a_cs.T
    else:
        a_cs_t = jnp.concatenate([a_cs, jnp.zeros((LANE - q, LANE), F32)], axis=0).T[:, :q]

    both = jnp.concatenate(_split3(jnp.concatenate([dt, a_cs], axis=0)), axis=1)
    both_e = _dot(both, expand_ref[...])
    dt_e = both_e[:q]
    acs_e = both_e[q:]
    from_h = jnp.exp2(acs_e)
    to_end = jnp.exp2(acs_e[q - 1:q, :] - acs_e)
    xdt = xs * dt_e
    xend = (xdt * to_end).astype(BF16)
    xdt_b = xdt.astype(BF16)

    li = lax.broadcasted_iota(jnp.int32, (q, q), 0)
    si = lax.broadcasted_iota(jnp.int32, (q, q), 1)
    causal = li >= si
    hp = SSD_HEADS // SSD_GROUPS
    gw = hp * SSD_HEADDIM
    for g in range(SSD_GROUPS):
        sl = slice(g * gw, (g + 1) * gw)
        bg = bm[:, g * SSD_STATE:(g + 1) * SSD_STATE].astype(BF16)
        cg = cm[:, g * SSD_STATE:(g + 1) * SSD_STATE].astype(BF16)
        cbm = lax.dot_general(cg, bg, (((1,), (1,)), ((), ())), preferred_element_type=F32)
        st_g = st_v[:, sl]
        y_g = _dot(cg, st_g.astype(BF16)) * from_h[:, sl]
        parts = []
        for r in range(hp):
            h = g * hp + r
            seg = a_cs[:, h:h + 1] - a_cs_t[h:h + 1, :]
            decay = jnp.exp2(jnp.where(causal, seg, -jnp.inf))
            m = (cbm * decay).astype(BF16)
            parts.append(_dot(m, xdt_b[:, h * SSD_HEADDIM:(h + 1) * SSD_HEADDIM]))
        y_g = y_g + jnp.concatenate(parts, axis=1)
        upd = lax.dot_general(bg, xend[:, sl], (((0,), (0,)), ((), ())),
                              preferred_element_type=F32)
        st_v[:, sl] = st_g * from_h[q - 1:q, sl] + upd
        y_g = y_g + dexp_ref[:, sl] * xs[:, sl]
        y_g = y_g * _silu(za_v[:, sl])
        y_v[:, sl] = _rmsnorm(y_g, ng_ref[:, sl])
        after_group(g)


def _ssd_kernel(*refs, q, bg, has_state):
    if has_state:
        h_ref, wza_ref, wxbc_ref, wdt_ref, convp_ref, h0_ref = refs[:6]
        refs = refs[6:]
    else:
        h_ref, wza_ref, wxbc_ref, wdt_ref = refs[:4]
        refs = refs[4:]
    prm = refs[:7]
    y_ref, convo_ref, ho_ref, za_s, xbc_s, dt_s, prev_ref, st_ref = refs[7:]
    c = pl.program_id(1)
    last = pl.num_programs(1) - 1
    k1 = SSD_CONV - 1

    def init_carry():
        prev_ref[...] = jnp.zeros(prev_ref.shape, F32)
        if has_state:
            prev_ref[:, SUBLANE - k1:, :] = convp_ref[...]
            for b in range(bg):
                st_ref[b] = h0_ref[b].T
        else:
            st_ref[...] = jnp.zeros(st_ref.shape, F32)

    @pl.when(c == 0)
    def _():
        init_carry()
        za_s[1] = jnp.zeros(za_s.shape[1:], F32)
        xbc_s[1] = jnp.zeros(xbc_s.shape[1:], F32)
        dt_s[1] = jnp.zeros(dt_s.shape[1:], F32)

    def step(slot):
        h = h_ref[...].reshape(bg * q, D_MODEL)
        pieces = [(dst, w_ref, c0, min(MXU_DIM, w_ref.shape[1] - c0))
                  for dst, w_ref in ((xbc_s, wxbc_ref), (za_s, wza_ref), (dt_s, wdt_ref))
                  for c0 in range(0, w_ref.shape[1], MXU_DIM)]

        def project(k):
            dst, w_ref, c0, w = pieces[k]
            dst[slot, :, :, c0:c0 + w] = _dot(h, w_ref[:, c0:c0 + w]).reshape(bg, q, w)

        slots = bg * SSD_GROUPS
        per_slot = -(-len(pieces) // slots)
        o = 1 - slot
        for b in range(bg):
            def after_group(g, b=b):
                k0 = (b * SSD_GROUPS + g) * per_slot
                for k in range(k0, min(k0 + per_slot, len(pieces))):
                    project(k)
            _ssd_chunk(za_s.at[o, b], xbc_s.at[o, b], dt_s.at[o, b], prev_ref.at[b],
                       st_ref.at[b], y_ref.at[b], prm, q, after_group)

    @pl.when(c % 2 == 0)
    def _():
        step(0)

    @pl.when(c % 2 == 1)
    def _():
        step(1)

    @pl.when(c == 0)
    def _():
        init_carry()

    @pl.when(c == last)
    def _():
        convo_ref[...] = prev_ref[:, SUBLANE - k1:, :]
        for b in range(bg):
            ho_ref[b] = st_ref[b].T


def _ssd(h, wza, wxbc, wdt, conv_prev, h0, cw, cb, dtb, alog, dexp, ng, expand, q, bg):
    bsz, length, _ = h.shape
    nc = length // q
    has_state = conv_prev is not None
    full = lambda shape: pl.BlockSpec(shape, lambda b, c: (0,) * len(shape))
    per_b = lambda shape: pl.BlockSpec((bg,) + shape, lambda b, c: (b, 0, 0))
    k1 = SSD_CONV - 1
    in_specs = [pl.BlockSpec((bg, q, D_MODEL), lambda b, c: (b, jnp.minimum(c, nc - 1), 0)),
                full(wza.shape), full(wxbc.shape), full(wdt.shape)]
    args = [h, wza, wxbc, wdt]
    if has_state:
        in_specs += [per_b((k1, SSD_CONV_DIM)), per_b((D_SSD, SSD_STATE))]
        args += [conv_prev, h0]
    params = [cw, cb, dtb, alog, dexp, ng, expand]
    in_specs += [full(p.shape) for p in params]
    args += params
    return pl.pallas_call(
        functools.partial(_ssd_kernel, q=q, bg=bg, has_state=has_state),
        grid=(bsz // bg, nc + 1),
        in_specs=in_specs,
        out_specs=[pl.BlockSpec((bg, q, D_SSD), lambda b, c: (b, jnp.maximum(c - 1, 0), 0)),
                   per_b((k1, SSD_CONV_DIM)), per_b((D_SSD, SSD_STATE))],
        out_shape=[jax.ShapeDtypeStruct((bsz, length, D_SSD), F32),
                   jax.ShapeDtypeStruct((bsz, k1, SSD_CONV_DIM), F32),
                   jax.ShapeDtypeStruct((bsz, D_SSD, SSD_STATE), F32)],
        scratch_shapes=[pltpu.VMEM((2, bg, q, D_SSD), F32),
                        pltpu.VMEM((2, bg, q, SSD_CONV_DIM), F32),
                        pltpu.VMEM((2, bg, q, DT_PAD), F32),
                        pltpu.VMEM((bg, SUBLANE, SSD_CONV_DIM), F32),
                        pltpu.VMEM((bg, SSD_STATE, D_SSD), F32)],
        compiler_params=pltpu.CompilerParams(
            dimension_semantics=("parallel", "arbitrary"), vmem_limit_bytes=VMEM_LIMIT),
        name="ssd",
    )(*args)


def _s5_kernel(*refs, bsz, tq, has_state):
    if has_state:
        (u_ref, zb_ref, h0r_ref, h0i_ref, lr_ref, li_ref, bre_ref, bim_ref, cre_ref, cim_ref,
         d_ref, wg_ref, bg_ref, y_ref, hro_ref, hio_ref, bur_ref, bui_ref, hr_ref, hi_ref) = refs
    else:
        (u_ref, zb_ref, lr_ref, li_ref, bre_ref, bim_ref, cre_ref, cim_ref,
         d_ref, wg_ref, bg_ref, y_ref, hro_ref, hio_ref, bur_ref, bui_ref, hr_ref, hi_ref) = refs
    i = pl.program_id(0)
    last = pl.num_programs(0) - 1

    @pl.when(i == 0)
    def _():
        if has_state:
            hr_ref[...] = h0r_ref[...]
            hi_ref[...] = h0i_ref[...]
        else:
            hr_ref[...] = jnp.zeros(hr_ref.shape, F32)
            hi_ref[...] = jnp.zeros(hi_ref.shape, F32)

    u = u_ref[...]
    u_b = u.astype(BF16)
    n_slab = S5_NT // S5_SLAB
    units = [(j, s) for j in range(S5_NJ) for s in range(n_slab)]

    def project_in(k):
        j, s = units[k]
        uj = u_b[:, j * S5_KT:(j + 1) * S5_KT]
        cols = slice(s * S5_SLAB, (s + 1) * S5_SLAB)
        bur_ref[k % 2] = _dot(uj, bre_ref[j, :, cols])
        bui_ref[k % 2] = _dot(uj, bim_ref[j, :, cols])

    y_parts = [None] * S5_NJ
    gate_parts = [None] * S5_NJ
    project_in(0)
    for k, (j, s) in enumerate(units):
        if k + 1 < len(units):
            project_in(k + 1)
        p = k % 2
        lo = j * S5_NT + s * S5_SLAB
        lam_r = jnp.broadcast_to(lr_ref[:, lo:lo + S5_SLAB], (bsz, S5_SLAB))
        lam_i = jnp.broadcast_to(li_ref[:, lo:lo + S5_SLAB], (bsz, S5_SLAB))
        hr = hr_ref[:, lo:lo + S5_SLAB]
        hi = hi_ref[:, lo:lo + S5_SLAB]
        for t in range(tq):
            rows = slice(t * bsz, (t + 1) * bsz)
            nr = lam_r * hr - lam_i * hi + bur_ref[p, rows, :]
            ni = lam_r * hi + lam_i * hr + bui_ref[p, rows, :]
            bur_ref[p, rows, :] = nr
            bui_ref[p, rows, :] = ni
            hr, hi = nr, ni
        hr_ref[:, lo:lo + S5_SLAB] = hr
        hi_ref[:, lo:lo + S5_SLAB] = hi
        crow = slice(s * S5_SLAB, (s + 1) * S5_SLAB)
        yk = (_dot(bur_ref[p].astype(BF16), cre_ref[j, crow, :])
              + _dot(bui_ref[p].astype(BF16), cim_ref[j, crow, :]))
        y_parts[j] = yk if y_parts[j] is None else y_parts[j] + yk
        if s == n_slab - 1:
            jc = slice(j * S5_KT, (j + 1) * S5_KT)
            y_parts[j] = _gelu_tanh(y_parts[j] + d_ref[:, jc] * u[:, jc])
            gate_parts[j] = y_parts[j] * _silu(zb_ref[:, jc])
    y_b = jnp.concatenate(y_parts, axis=1).astype(BF16)
    for n in range(S5_NJ):
        nc = slice(n * S5_KT, (n + 1) * S5_KT)
        z = _dot(y_b, wg_ref[:, nc]) + bg_ref[:, nc]
        y_ref[:, nc] = gate_parts[n] * _sigmoid(z)

    @pl.when(i == last)
    def _():
        hro_ref[...] = hr_ref[...]
        hio_ref[...] = hi_ref[...]


def _s5(u, zb, h0r, h0i, lr, li, bre, bim, cre, cim, d, wg, bg, bsz, m):
    rows = u.shape[0]
    has_state = h0r is not None
    tq = m // bsz
    blk = pl.BlockSpec((m, D_S5), lambda i: (i, 0))
    full = lambda shape: pl.BlockSpec(shape, lambda i: (0,) * len(shape))
    in_specs = [blk, blk]
    args = [u, zb]
    if has_state:
        in_specs += [full(h0r.shape), full(h0i.shape)]
        args += [h0r, h0i]
    params = [lr, li, bre, bim, cre, cim, d, wg, bg]
    in_specs += [full(p.shape) for p in params]
    args += params
    st = jax.ShapeDtypeStruct((bsz, S5_LANES), F32)
    return pl.pallas_call(
        functools.partial(_s5_kernel, bsz=bsz, tq=tq, has_state=has_state),
        grid=(rows // m,),
        in_specs=in_specs,
        out_specs=[blk, full((bsz, S5_LANES)), full((bsz, S5_LANES))],
        out_shape=[jax.ShapeDtypeStruct((rows, D_S5), F32), st, st],
        scratch_shapes=[pltpu.VMEM((2, m, S5_SLAB), F32), pltpu.VMEM((2, m, S5_SLAB), F32),
                        pltpu.VMEM((bsz, S5_LANES), F32), pltpu.VMEM((bsz, S5_LANES), F32)],
        compiler_params=pltpu.CompilerParams(
            dimension_semantics=("arbitrary",), vmem_limit_bytes=VMEM_LIMIT),
        name="s5",
    )(*args)


def _outproj_kernel(*refs, bsz, tq, tp, final):
    if final:
        x_ref, ya_ref, yb_ref, pt_ref, w_ref, g_ref, o_ref = refs
    else:
        x_ref, ya_ref, yb_ref, pt_ref, w_ref, o_ref = refs
    m = bsz * tq
    mp = bsz * tp
    yb = yb_ref[...].astype(BF16)
    parts = [_dot(pt_ref[...], yb[s * mp:(s + 1) * mp, :]).astype(BF16).reshape(bsz, tp, D_S5)
             for s in range(tq // tp)]
    yb_b = jnp.concatenate(parts, axis=1).reshape(m, D_S5)
    ya = ya_ref[...].reshape(m, D_SSD).astype(BF16)
    x = x_ref[...].reshape(m, D_MODEL) + _dot(jnp.concatenate([ya, yb_b], axis=1), w_ref[...])
    if final:
        x = _rmsnorm(x, g_ref[...])
    o_ref[...] = x.reshape(bsz, tq, D_MODEL)


def _outproj(x, ya, yb, perm_t, w, final_g, tq, tp):
    bsz, length, _ = x.shape
    final = final_g is not None
    bmaj = pl.BlockSpec((bsz, tq, D_MODEL), lambda i: (0, i, 0))
    tmaj = pl.BlockSpec((bsz * tq, D_S5), lambda i: (i, 0))
    full = lambda shape: pl.BlockSpec(shape, lambda i: (0,) * len(shape))
    in_specs = [bmaj, bmaj, tmaj, full(perm_t.shape), full(w.shape)]
    args = [x, ya, yb, perm_t, w]
    if final:
        in_specs.append(full((1, D_MODEL)))
        args.append(final_g)
    return pl.pallas_call(
        functools.partial(_outproj_kernel, bsz=bsz, tq=tq, tp=tp, final=final),
        grid=(length // tq,),
        in_specs=in_specs,
        out_specs=bmaj,
        out_shape=jax.ShapeDtypeStruct(x.shape, F32),
        compiler_params=pltpu.CompilerParams(
            dimension_semantics=("parallel",), vmem_limit_bytes=VMEM_LIMIT),
        name="outproj",
    )(*args)


def _prep_layer(l, norm_g, w_in, conv_w, conv_b, dt_bias, a_log, d_ssd, ssd_norm_g,
                lam_re, lam_im, log_dt, b_re, b_im, c_re, c_im, d_s5, w_glu, b_glu, w_out):
    w = w_in[l]
    o1 = D_SSD
    o2 = o1 + SSD_CONV_DIM
    o3 = o2 + SSD_HEADS
    o4 = o3 + D_S5
    wdt = jnp.zeros((D_MODEL, DT_PAD), F32).at[:, :SSD_HEADS].set(w[:, o2:o3])
    pad_h = lambda v: jnp.zeros((1, DT_PAD), F32).at[0, :SSD_HEADS].set(v)
    lam = lax.complex(lam_re[l], lam_im[l])
    lam_bar = jnp.exp(lam * jnp.exp(log_dt[l])[:, None])
    b_bar = ((lam_bar - 1.0) / lam)[..., None] * lax.complex(b_re[l], b_im[l])
    gpt = S5_KT // S5_GROUP_CH
    same_group = ((jnp.arange(S5_KT) // S5_GROUP_CH)[:, None]
                  == (jnp.arange(S5_NT) // S5_STATE)[None, :])

    def blockdiag_b(v):
        v = jnp.transpose(v, (0, 2, 1)).reshape(S5_NJ, S5_KT, S5_STATE)
        return jnp.where(same_group, jnp.tile(v, (1, 1, gpt)), 0.0).astype(BF16)

    def blockdiag_c(v):
        v = jnp.transpose(v, (0, 2, 1)).reshape(S5_NJ, S5_NT, S5_GROUP_CH)
        return jnp.where(same_group.T, jnp.tile(v, (1, 1, gpt)), 0.0).astype(BF16)

    return dict(
        norm_g=norm_g[l][None, :],
        wza=w[:, :o1].astype(BF16), wxbc=w[:, o1:o2].astype(BF16), wdt=wdt.astype(BF16),
        wu=w[:, o3:o4].astype(BF16), wzb=w[:, o4:].astype(BF16),
        cw=conv_w[l], cb=conv_b[l][None, :], dtb=pad_h(dt_bias[l]), alog=pad_h(a_log[l]),
        dexp=jnp.repeat(d_ssd[l], SSD_HEADDIM)[None, :], ng=ssd_norm_g[l][None, :],
        lr=jnp.real(lam_bar).reshape(1, S5_LANES), li=jnp.imag(lam_bar).reshape(1, S5_LANES),
        bre=blockdiag_b(jnp.real(b_bar)), bim=blockdiag_b(jnp.imag(b_bar)),
        cre=blockdiag_c(c_re[l]), cim=blockdiag_c(-c_im[l]),
        d5=d_s5[l][None, :], wg=w_glu[l].astype(BF16), bg=b_glu[l][None, :],
        wout=w_out[l].astype(BF16),
    )


def _trunk(x, states, layers, final_g, expand):
    bsz, length, _ = x.shape
    tl = _tiles(bsz, length)
    tq, tp = tl['tq'], tl['tp']
    r = jnp.arange(bsz * tp)
    perm = (((r % bsz) * tp + r // bsz)[:, None] == r[None, :]).astype(BF16)
    convs, ssds, res, ims = [], [], [], []
    for l, p in enumerate(layers):
        h, u, zb = _inproj(x, p['norm_g'], perm, p['wu'], p['wzb'], tq, tp)
        if states is None:
            conv_prev = h0 = h0r = h0i = None
        else:
            conv_s, ssd_s, s5re_s, s5im_s = states
            conv_prev = conv_s[l]
            h0 = ssd_s[l].reshape(bsz, D_SSD, SSD_STATE)
            h0r = s5re_s[l].reshape(bsz, S5_LANES)
            h0i = s5im_s[l].reshape(bsz, S5_LANES)
        ya, conv_new, ssd_new = _ssd(h, p['wza'], p['wxbc'], p['wdt'], conv_prev, h0,
                                     p['cw'], p['cb'], p['dtb'], p['alog'], p['dexp'], p['ng'],
                                     expand, tl['q'], tl['bg'])
        yb, hr, hi = _s5(u, zb, h0r, h0i, p['lr'], p['li'], p['bre'], p['bim'],
                         p['cre'], p['cim'], p['d5'], p['wg'], p['bg'], bsz, tl['m'])
        x = _outproj(x, ya, yb, perm.T, p['wout'], final_g if l == DEPTH - 1 else None, tq, tp)
        convs.append(conv_new)
        ssds.append(ssd_new.reshape(bsz, SSD_HEADS, SSD_HEADDIM, SSD_STATE))
        res.append(hr.reshape(bsz, S5_GROUPS, S5_STATE))
        ims.append(hi.reshape(bsz, S5_GROUPS, S5_STATE))
    return x, jnp.stack(convs), jnp.stack(ssds), jnp.stack(res), jnp.stack(ims)


def kernel(x_prompt, x_sample, state_ssd_conv, state_ssd, state_s5_re, state_s5_im, norm_g, w_in, conv_w, conv_b, dt_bias, a_log, d_ssd, ssd_norm_g, lam_re, lam_im, log_dt, b_re, b_im, c_re, c_im, d_s5, w_glu, b_glu, w_out, final_norm_g):
    layers = [_prep_layer(l, norm_g, w_in, conv_w, conv_b, dt_bias, a_log, d_ssd, ssd_norm_g,
                          lam_re, lam_im, log_dt, b_re, b_im, c_re, c_im, d_s5, w_glu, b_glu,
                          w_out) for l in range(DEPTH)]
    final_g = final_norm_g[None, :]
    e = (jnp.arange(DT_PAD)[:, None] == (jnp.arange(D_SSD) // SSD_HEADDIM)[None, :]).astype(BF16)
    expand = jnp.concatenate([e, e, e], axis=0)
    yp, conv_p, ssd_p, re_p, im_p = _trunk(x_prompt, None, layers, final_g, expand)
    ys, conv_s, ssd_s, re_s, im_s = _trunk(
        x_sample, (state_ssd_conv, state_ssd, state_s5_re, state_s5_im), layers, final_g, expand)
    return (yp, ys, conv_p, ssd_p, re_p, im_p, conv_s, ssd_s, re_s, im_s)
```

```python
import functools
import math

import jax
import jax.numpy as jnp
from jax import lax
from jax.experimental import pallas as pl
from jax.experimental.pallas import tpu as pltpu

D_MODEL = 1024
DEPTH = 4
D_SSD = 1024
D_S5 = 1024
SSD_HEADDIM = 64
SSD_HEADS = 16
SSD_GROUPS = 4
SSD_STATE = 128
SSD_CONV = 4
SSD_CONV_DIM = D_SSD + 2 * SSD_GROUPS * SSD_STATE
S5_GROUP_CH = 16
S5_GROUPS = 64
S5_STATE = 64
S5_LANES = S5_GROUPS * S5_STATE
EPS = 1e-6

LANE = 128
SUBLANE = 8
MXU_DIM = 256
DT_PAD = LANE
S5_KT = MXU_DIM
S5_NT = S5_KT // S5_GROUP_CH * S5_STATE
S5_NJ = D_S5 // S5_KT
S5_SLAB = 512
SSD_BG = 4
VMEM_LIMIT = 56 * 1024 * 1024

F32 = jnp.float32
BF16 = jnp.bfloat16


def _tiles(bsz, length):
    tp = MXU_DIM // bsz
    tq = max(tp, min(length, 512 // bsz))
    q = min(length, LANE)
    return dict(tq=tq, tp=tp, q=q, m=bsz * tq, bg=SSD_BG)


LOG2E = math.log2(math.e)


def _sigmoid(x):
    return 1.0 / (1.0 + jnp.exp2(x * -LOG2E))


def _silu(x):
    return x * _sigmoid(x)


def _softplus(x):
    return jnp.maximum(x, 0.0) + jnp.log1p(jnp.exp(-jnp.abs(x)))


def _gelu_tanh(x):
    c = math.sqrt(2.0 / math.pi)
    return 0.5 * x * (1.0 + jnp.tanh(c * (x + 0.044715 * (x * x * x))))


def _dot(a, b):
    return jnp.dot(a, b, preferred_element_type=F32)


def _rmsnorm(x, g):
    ms = jnp.mean(x * x, axis=-1, keepdims=True)
    return (x * lax.rsqrt(ms + EPS)) * g


def _in_rows(x, g_ref, p_ref, wu_ref, wzb_ref, h_ref, u_ref, zb_ref, bsz, tq, tp):
    h = _rmsnorm(x, g_ref[...]).astype(BF16)
    h3 = h.reshape(bsz, tq, D_MODEL)
    h_ref[...] = h3
    parts = [_dot(p_ref[...], h3[:, s * tp:(s + 1) * tp, :].reshape(bsz * tp, D_MODEL)).astype(BF16)
             for s in range(tq // tp)]
    h_t = jnp.concatenate(parts, axis=0)
    u_ref[...] = _dot(h_t, wu_ref[...])
    zb_ref[...] = _dot(h_t, wzb_ref[...])


def _inproj_kernel(x_ref, g_ref, p_ref, wu_ref, wzb_ref, h_ref, u_ref, zb_ref, *, bsz, tq, tp):
    x = x_ref[...].reshape(bsz * tq, D_MODEL)
    _in_rows(x, g_ref, p_ref, wu_ref, wzb_ref, h_ref, u_ref, zb_ref, bsz, tq, tp)


def _inproj(x, g, perm, wu, wzb, tq, tp):
    bsz, length, _ = x.shape
    m = bsz * tq
    full = lambda shape: pl.BlockSpec(shape, lambda i: (0,) * len(shape))
    tmaj = pl.BlockSpec((m, D_S5), lambda i: (i, 0))
    bmaj = pl.BlockSpec((bsz, tq, D_MODEL), lambda i: (0, i, 0))
    return pl.pallas_call(
        functools.partial(_inproj_kernel, bsz=bsz, tq=tq, tp=tp),
        grid=(length // tq,),
        in_specs=[bmaj, full((1, D_MODEL)), full(perm.shape), full(wu.shape), full(wzb.shape)],
        out_specs=[bmaj, tmaj, tmaj],
        out_shape=[jax.ShapeDtypeStruct(x.shape, BF16)]
                  + [jax.ShapeDtypeStruct((length * bsz, D_S5), F32)] * 2,
        compiler_params=pltpu.CompilerParams(
            dimension_semantics=("parallel",), vmem_limit_bytes=VMEM_LIMIT),
        name="inproj",
    )(x, g, perm, wu, wzb)


def _cumsum_rows(x):
    n = x.shape[0]
    row = lax.broadcasted_iota(jnp.int32, x.shape, 0)
    s = 1
    while s < n:
        x = x + jnp.where(row >= s, pltpu.roll(x, s, axis=0), 0.0)
        s *= 2
    return x


def _split3(x):
    b0 = x.astype(BF16)
    r = x - b0.astype(F32)
    b1 = r.astype(BF16)
    b2 = (r - b1.astype(F32)).astype(BF16)
    return b0, b1, b2


def _ssd_chunk(za_v, xbc_v, dt_v, prev_v, st_v, y_v, prm, q, after_group):
    cw_ref, cb_ref, dtb_ref, alog_ref, dexp_ref, ng_ref, expand_ref = prm
    k1 = SSD_CONV - 1
    x_cur = xbc_v[...]
    xx = jnp.concatenate([prev_v[...], x_cur], axis=0)
    conv = cb_ref[...] + cw_ref[k1:k1 + 1, :] * x_cur
    for s in range(1, SSD_CONV):
        conv = conv + cw_ref[k1 - s:k1 - s + 1, :] * pltpu.roll(xx, s, axis=0)[SUBLANE:]
    prev_v[...] = x_cur[q - SUBLANE:]

    act = _silu(conv)
    xs = act[:, :D_SSD]
    bm = act[:, D_SSD:D_SSD + SSD_GROUPS * SSD_STATE]
    cm = act[:, D_SSD + SSD_GROUPS * SSD_STATE:]

    dt = _softplus(dt_v[...] + dtb_ref[...])
    a = dt * (-jnp.exp(alog_ref[...]))
    a_cs = _cumsum_rows(a) * LOG2E
    if q % LANE == 0:
        a_cs_t = a_cs.T
    else:
        a_cs_t = jnp.concatenate([a_cs, jnp.zeros((LANE - q, LANE), F32)], axis=0).T[:, :q]

    both = jnp.concatenate(_split3(jnp.concatenate([dt, a_cs], axis=0)), axis=1)
    both_e = _dot(both, expand_ref[...])
    dt_e = both_e[:q]
    acs_e = both_e[q:]
    from_h = jnp.exp2(acs_e)
    to_end = jnp.exp2(acs_e[q - 1:q, :] - acs_e)
    xdt = xs * dt_e
    xend = (xdt * to_end).astype(BF16)
    xdt_b = xdt.astype(BF16)

    li = lax.broadcasted_iota(jnp.int32, (q, q), 0)
    si = lax.broadcasted_iota(jnp.int32, (q, q), 1)
    causal = li >= si
    hp = SSD_HEADS // SSD_GROUPS
    gw = hp * SSD_HEADDIM
    for g in range(SSD_GROUPS):
        sl = slice(g * gw, (g + 1) * gw)
        bg = bm[:, g * SSD_STATE:(g + 1) * SSD_STATE].astype(BF16)
        cg = cm[:, g * SSD_STATE:(g + 1) * SSD_STATE].astype(BF16)
        cbm = lax.dot_general(cg, bg, (((1,), (1,)), ((), ())), preferred_element_type=F32)
        st_g = st_v[:, sl]
        y_g = _dot(cg, st_g.astype(BF16)) * from_h[:, sl]
        parts = []
        for r in range(hp):
            h = g * hp + r
            seg = a_cs[:, h:h + 1] - a_cs_t[h:h + 1, :]
            decay = jnp.exp2(jnp.where(causal, seg, -jnp.inf))
            m = (cbm * decay).astype(BF16)
            parts.append(_dot(m, xdt_b[:, h * SSD_HEADDIM:(h + 1) * SSD_HEADDIM]))
        y_g = y_g + jnp.concatenate(parts, axis=1)
        upd = lax.dot_general(bg, xend[:, sl], (((0,), (0,)), ((), ())),
                              preferred_element_type=F32)
        st_v[:, sl] = st_g * from_h[q - 1:q, sl] + upd
        y_g = y_g + dexp_ref[:, sl] * xs[:, sl]
        y_g = y_g * _silu(za_v[:, sl])
        y_v[:, sl] = _rmsnorm(y_g, ng_ref[:, sl]).astype(y_v.dtype)
        after_group(g)


def _ssd_kernel(*refs, q, bg, has_state):
    if has_state:
        h_ref, wza_ref, wxbc_ref, wdt_ref, convp_ref, h0_ref = refs[:6]
        refs = refs[6:]
    else:
        h_ref, wza_ref, wxbc_ref, wdt_ref = refs[:4]
        refs = refs[4:]
    prm = refs[:7]
    y_ref, convo_ref, ho_ref, za_s, xbc_s, dt_s, prev_ref, st_ref = refs[7:]
    c = pl.program_id(1)
    last = pl.num_programs(1) - 1
    k1 = SSD_CONV - 1

    def init_carry():
        prev_ref[...] = jnp.zeros(prev_ref.shape, F32)
        if has_state:
            prev_ref[:, SUBLANE - k1:, :] = convp_ref[...]
            for b in range(bg):
                st_ref[b] = h0_ref[b].T
        else:
            st_ref[...] = jnp.zeros(st_ref.shape, F32)

    @pl.when(c == 0)
    def _():
        init_carry()
        za_s[1] = jnp.zeros(za_s.shape[1:], F32)
        xbc_s[1] = jnp.zeros(xbc_s.shape[1:], F32)
        dt_s[1] = jnp.zeros(dt_s.shape[1:], F32)

    def step(slot):
        h = h_ref[...].reshape(bg * q, D_MODEL)
        pieces = [(dst, w_ref, c0, min(MXU_DIM, w_ref.shape[1] - c0))
                  for dst, w_ref in ((xbc_s, wxbc_ref), (za_s, wza_ref), (dt_s, wdt_ref))
                  for c0 in range(0, w_ref.shape[1], MXU_DIM)]

        def project(k):
            dst, w_ref, c0, w = pieces[k]
            dst[slot, :, :, c0:c0 + w] = _dot(h, w_ref[:, c0:c0 + w]).reshape(bg, q, w)

        slots = bg * SSD_GROUPS
        per_slot = -(-len(pieces) // slots)
        o = 1 - slot
        for b in range(bg):
            def after_group(g, b=b):
                k0 = (b * SSD_GROUPS + g) * per_slot
                for k in range(k0, min(k0 + per_slot, len(pieces))):
                    project(k)
            _ssd_chunk(za_s.at[o, b], xbc_s.at[o, b], dt_s.at[o, b], prev_ref.at[b],
                       st_ref.at[b], y_ref.at[b], prm, q, after_group)

    @pl.when(c % 2 == 0)
    def _():
        step(0)

    @pl.when(c % 2 == 1)
    def _():
        step(1)

    @pl.when(c == 0)
    def _():
        init_carry()

    @pl.when(c == last)
    def _():
        convo_ref[...] = prev_ref[:, SUBLANE - k1:, :]
        for b in range(bg):
            ho_ref[b] = st_ref[b].T


def _ssd(h, wza, wxbc, wdt, conv_prev, h0, cw, cb, dtb, alog, dexp, ng, expand, q, bg):
    bsz, length, _ = h.shape
    nc = length // q
    has_state = conv_prev is not None
    full = lambda shape: pl.BlockSpec(shape, lambda b, c: (0,) * len(shape))
    per_b = lambda shape: pl.BlockSpec((bg,) + shape, lambda b, c: (b, 0, 0))
    k1 = SSD_CONV - 1
    in_specs = [pl.BlockSpec((bg, q, D_MODEL), lambda b, c: (b, jnp.minimum(c, nc - 1), 0)),
                full(wza.shape), full(wxbc.shape), full(wdt.shape)]
    args = [h, wza, wxbc, wdt]
    if has_state:
        in_specs += [per_b((k1, SSD_CONV_DIM)), per_b((D_SSD, SSD_STATE))]
        args += [conv_prev, h0]
    params = [cw, cb, dtb, alog, dexp, ng, expand]
    in_specs += [full(p.shape) for p in params]
    args += params
    return pl.pallas_call(
        functools.partial(_ssd_kernel, q=q, bg=bg, has_state=has_state),
        grid=(bsz // bg, nc + 1),
        in_specs=in_specs,
        out_specs=[pl.BlockSpec((bg, q, D_SSD), lambda b, c: (b, jnp.maximum(c - 1, 0), 0)),
                   per_b((k1, SSD_CONV_DIM)), per_b((D_SSD, SSD_STATE))],
        out_shape=[jax.ShapeDtypeStruct((bsz, length, D_SSD), BF16),
                   jax.ShapeDtypeStruct((bsz, k1, SSD_CONV_DIM), F32),
                   jax.ShapeDtypeStruct((bsz, D_SSD, SSD_STATE), F32)],
        scratch_shapes=[pltpu.VMEM((2, bg, q, D_SSD), F32),
                        pltpu.VMEM((2, bg, q, SSD_CONV_DIM), F32),
                        pltpu.VMEM((2, bg, q, DT_PAD), F32),
                        pltpu.VMEM((bg, SUBLANE, SSD_CONV_DIM), F32),
                        pltpu.VMEM((bg, SSD_STATE, D_SSD), F32)],
        compiler_params=pltpu.CompilerParams(
            dimension_semantics=("parallel", "arbitrary"), vmem_limit_bytes=VMEM_LIMIT),
        name="ssd",
    )(*args)


def _s5_kernel(*refs, bsz, tq, has_state):
    if has_state:
        (u_ref, zb_ref, h0r_ref, h0i_ref, lr_ref, li_ref, bre_ref, bim_ref, cre_ref, cim_ref,
         d_ref, wg_ref, bg_ref, y_ref, hro_ref, hio_ref, bur_ref, bui_ref, hr_ref, hi_ref) = refs
    else:
        (u_ref, zb_ref, lr_ref, li_ref, bre_ref, bim_ref, cre_ref, cim_ref,
         d_ref, wg_ref, bg_ref, y_ref, hro_ref, hio_ref, bur_ref, bui_ref, hr_ref, hi_ref) = refs
    i = pl.program_id(0)
    last = pl.num_programs(0) - 1

    @pl.when(i == 0)
    def _():
        if has_state:
            hr_ref[...] = h0r_ref[...]
            hi_ref[...] = h0i_ref[...]
        else:
            hr_ref[...] = jnp.zeros(hr_ref.shape, F32)
            hi_ref[...] = jnp.zeros(hi_ref.shape, F32)

    u = u_ref[...]
    u_b = u.astype(BF16)
    n_slab = S5_NT // S5_SLAB
    units = [(j, s) for j in range(S5_NJ) for s in range(n_slab)]

    def project_in(k):
        j, s = units[k]
        uj = u_b[:, j * S5_KT:(j + 1) * S5_KT]
        cols = slice(s * S5_SLAB, (s + 1) * S5_SLAB)
        bur_ref[k % 2] = _dot(uj, bre_ref[j, :, cols])
        bui_ref[k % 2] = _dot(uj, bim_ref[j, :, cols])

    y_parts = [None] * S5_NJ
    gate_parts = [None] * S5_NJ
    project_in(0)
    for k, (j, s) in enumerate(units):
        if k + 1 < len(units):
            project_in(k + 1)
        p = k % 2
        lo = j * S5_NT + s * S5_SLAB
        lam_r = jnp.broadcast_to(lr_ref[:, lo:lo + S5_SLAB], (bsz, S5_SLAB))
        lam_i = jnp.broadcast_to(li_ref[:, lo:lo + S5_SLAB], (bsz, S5_SLAB))
        hr = hr_ref[:, lo:lo + S5_SLAB]
        hi = hi_ref[:, lo:lo + S5_SLAB]
        for t in range(tq):
            rows = slice(t * bsz, (t + 1) * bsz)
            nr = lam_r * hr - lam_i * hi + bur_ref[p, rows, :]
            ni = lam_r * hi + lam_i * hr + bui_ref[p, rows, :]
            bur_ref[p, rows, :] = nr
            bui_ref[p, rows, :] = ni
            hr, hi = nr, ni
        hr_ref[:, lo:lo + S5_SLAB] = hr
        hi_ref[:, lo:lo + S5_SLAB] = hi
        crow = slice(s * S5_SLAB, (s + 1) * S5_SLAB)
        yk = (_dot(bur_ref[p].astype(BF16), cre_ref[j, crow, :])
              + _dot(bui_ref[p].astype(BF16), cim_ref[j, crow, :]))
        y_parts[j] = yk if y_parts[j] is None else y_parts[j] + yk
        if s == n_slab - 1:
            jc = slice(j * S5_KT, (j + 1) * S5_KT)
            y_parts[j] = _gelu_tanh(y_parts[j] + d_ref[:, jc] * u[:, jc])
            gate_parts[j] = y_parts[j] * _silu(zb_ref[:, jc])
    y_b = jnp.concatenate(y_parts, axis=1).astype(BF16)
    for n in range(S5_NJ):
        nc = slice(n * S5_KT, (n + 1) * S5_KT)
        z = _dot(y_b, wg_ref[:, nc]) + bg_ref[:, nc]
        y_ref[:, nc] = (gate_parts[n] * _sigmoid(z)).astype(y_ref.dtype)

    @pl.when(i == last)
    def _():
        hro_ref[...] = hr_ref[...]
        hio_ref[...] = hi_ref[...]


def _s5(u, zb, h0r, h0i, lr, li, bre, bim, cre, cim, d, wg, bg, bsz, m):
    rows = u.shape[0]
    has_state = h0r is not None
    tq = m // bsz
    blk = pl.BlockSpec((m, D_S5), lambda i: (i, 0))
    full = lambda shape: pl.BlockSpec(shape, lambda i: (0,) * len(shape))
    in_specs = [blk, blk]
    args = [u, zb]
    if has_state:
        in_specs += [full(h0r.shape), full(h0i.shape)]
        args += [h0r, h0i]
    params = [lr, li, bre, bim, cre, cim, d, wg, bg]
    in_specs += [full(p.shape) for p in params]
    args += params
    st = jax.ShapeDtypeStruct((bsz, S5_LANES), F32)
    return pl.pallas_call(
        functools.partial(_s5_kernel, bsz=bsz, tq=tq, has_state=has_state),
        grid=(rows // m,),
        in_specs=in_specs,
        out_specs=[blk, full((bsz, S5_LANES)), full((bsz, S5_LANES))],
        out_shape=[jax.ShapeDtypeStruct((rows, D_S5), BF16), st, st],
        scratch_shapes=[pltpu.VMEM((2, m, S5_SLAB), F32), pltpu.VMEM((2, m, S5_SLAB), F32),
                        pltpu.VMEM((bsz, S5_LANES), F32), pltpu.VMEM((bsz, S5_LANES), F32)],
        compiler_params=pltpu.CompilerParams(
            dimension_semantics=("arbitrary",), vmem_limit_bytes=VMEM_LIMIT),
        name="s5",
    )(*args)


def _out_rows(x_ref, ya_ref, yb_ref, pt_ref, w_ref, bsz, tq, tp):
    m = bsz * tq
    mp = bsz * tp
    yb = yb_ref[...]
    parts = [_dot(pt_ref[...], yb[s * mp:(s + 1) * mp, :]).astype(BF16).reshape(bsz, tp, D_S5)
             for s in range(tq // tp)]
    yb_b = jnp.concatenate(parts, axis=1).reshape(m, D_S5)
    ya = ya_ref[...].reshape(m, D_SSD)
    return x_ref[...].reshape(m, D_MODEL) + _dot(jnp.concatenate([ya, yb_b], axis=1), w_ref[...])


def _outproj_kernel(x_ref, ya_ref, yb_ref, pt_ref, w_ref, g_ref, o_ref, *, bsz, tq, tp):
    x = _out_rows(x_ref, ya_ref, yb_ref, pt_ref, w_ref, bsz, tq, tp)
    o_ref[...] = _rmsnorm(x, g_ref[...]).reshape(bsz, tq, D_MODEL)


def _outin_kernel(x_ref, ya_ref, yb_ref, pt_ref, w_ref, g_ref, p_ref, wu_ref, wzb_ref,
                  o_ref, h_ref, u_ref, zb_ref, *, bsz, tq, tp):
    x = _out_rows(x_ref, ya_ref, yb_ref, pt_ref, w_ref, bsz, tq, tp)
    o_ref[...] = x.reshape(bsz, tq, D_MODEL)
    _in_rows(x, g_ref, p_ref, wu_ref, wzb_ref, h_ref, u_ref, zb_ref, bsz, tq, tp)


def _outproj(x, ya, yb, perm_t, w, g, tq, tp, nxt=None):
    bsz, length, _ = x.shape
    m = bsz * tq
    bmaj = pl.BlockSpec((bsz, tq, D_MODEL), lambda i: (0, i, 0))
    tmaj = pl.BlockSpec((m, D_S5), lambda i: (i, 0))
    full = lambda shape: pl.BlockSpec(shape, lambda i: (0,) * len(shape))
    args = [x, ya, yb, perm_t, w, g]
    in_specs = [bmaj, bmaj, tmaj, full(perm_t.shape), full(w.shape), full(g.shape)]
    x_out = jax.ShapeDtypeStruct(x.shape, F32)
    if nxt is None:
        body, out_specs, out_shape = _outproj_kernel, bmaj, x_out
    else:
        args += list(nxt)
        in_specs += [full(a.shape) for a in nxt]
        body = _outin_kernel
        out_specs = [bmaj, bmaj, tmaj, tmaj]
        out_shape = [x_out, jax.ShapeDtypeStruct(x.shape, BF16)] \
            + [jax.ShapeDtypeStruct((length * bsz, D_S5), F32)] * 2
    return pl.pallas_call(
        functools.partial(body, bsz=bsz, tq=tq, tp=tp),
        grid=(length // tq,),
        in_specs=in_specs,
        out_specs=out_specs,
        out_shape=out_shape,
        compiler_params=pltpu.CompilerParams(
            dimension_semantics=("parallel",), vmem_limit_bytes=VMEM_LIMIT),
        name="outproj",
    )(*args)


def _prep_layer(l, norm_g, w_in, conv_w, conv_b, dt_bias, a_log, d_ssd, ssd_norm_g,
                lam_re, lam_im, log_dt, b_re, b_im, c_re, c_im, d_s5, w_glu, b_glu, w_out):
    w = w_in[l]
    o1 = D_SSD
    o2 = o1 + SSD_CONV_DIM
    o3 = o2 + SSD_HEADS
    o4 = o3 + D_S5
    wdt = jnp.zeros((D_MODEL, DT_PAD), F32).at[:, :SSD_HEADS].set(w[:, o2:o3])
    pad_h = lambda v: jnp.zeros((1, DT_PAD), F32).at[0, :SSD_HEADS].set(v)
    lam = lax.complex(lam_re[l], lam_im[l])
    lam_bar = jnp.exp(lam * jnp.exp(log_dt[l])[:, None])
    b_bar = ((lam_bar - 1.0) / lam)[..., None] * lax.complex(b_re[l], b_im[l])
    gpt = S5_KT // S5_GROUP_CH
    same_group = ((jnp.arange(S5_KT) // S5_GROUP_CH)[:, None]
                  == (jnp.arange(S5_NT) // S5_STATE)[None, :])

    def blockdiag_b(v):
        v = jnp.transpose(v, (0, 2, 1)).reshape(S5_NJ, S5_KT, S5_STATE)
        return jnp.where(same_group, jnp.tile(v, (1, 1, gpt)), 0.0).astype(BF16)

    def blockdiag_c(v):
        v = jnp.transpose(v, (0, 2, 1)).reshape(S5_NJ, S5_NT, S5_GROUP_CH)
        return jnp.where(same_group.T, jnp.tile(v, (1, 1, gpt)), 0.0).astype(BF16)

    return dict(
        norm_g=norm_g[l][None, :],
        wza=w[:, :o1].astype(BF16), wxbc=w[:, o1:o2].astype(BF16), wdt=wdt.astype(BF16),
        wu=w[:, o3:o4].astype(BF16), wzb=w[:, o4:].astype(BF16),
        cw=conv_w[l], cb=conv_b[l][None, :], dtb=pad_h(dt_bias[l]), alog=pad_h(a_log[l]),
        dexp=jnp.repeat(d_ssd[l], SSD_HEADDIM)[None, :], ng=ssd_norm_g[l][None, :],
        lr=jnp.real(lam_bar).reshape(1, S5_LANES), li=jnp.imag(lam_bar).reshape(1, S5_LANES),
        bre=blockdiag_b(jnp.real(b_bar)), bim=blockdiag_b(jnp.imag(b_bar)),
        cre=blockdiag_c(c_re[l]), cim=blockdiag_c(-c_im[l]),
        d5=d_s5[l][None, :], wg=w_glu[l].astype(BF16), bg=b_glu[l][None, :],
        wout=w_out[l].astype(BF16),
    )


def _trunk(x, states, layers, final_g, expand):
    bsz, length, _ = x.shape
    tl = _tiles(bsz, length)
    tq, tp = tl['tq'], tl['tp']
    r = jnp.arange(bsz * tp)
    perm = (((r % bsz) * tp + r // bsz)[:, None] == r[None, :]).astype(BF16)
    convs, ssds, res, ims = [], [], [], []
    h, u, zb = _inproj(x, layers[0]['norm_g'], perm, layers[0]['wu'], layers[0]['wzb'], tq, tp)
    for l, p in enumerate(layers):
        if states is None:
            conv_prev = h0 = h0r = h0i = None
        else:
            conv_s, ssd_s, s5re_s, s5im_s = states
            conv_prev = conv_s[l]
            h0 = ssd_s[l].reshape(bsz, D_SSD, SSD_STATE)
            h0r = s5re_s[l].reshape(bsz, S5_LANES)
            h0i = s5im_s[l].reshape(bsz, S5_LANES)
        ya, conv_new, ssd_new = _ssd(h, p['wza'], p['wxbc'], p['wdt'], conv_prev, h0,
                                     p['cw'], p['cb'], p['dtb'], p['alog'], p['dexp'], p['ng'],
                                     expand, tl['q'], tl['bg'])
        yb, hr, hi = _s5(u, zb, h0r, h0i, p['lr'], p['li'], p['bre'], p['bim'],
                         p['cre'], p['cim'], p['d5'], p['wg'], p['bg'], bsz, tl['m'])
        if l + 1 < len(layers):
            n = layers[l + 1]
            x, h, u, zb = _outproj(x, ya, yb, perm.T, p['wout'], n['norm_g'], tq, tp,
                                   nxt=(perm, n['wu'], n['wzb']))
        else:
            x = _outproj(x, ya, yb, perm.T, p['wout'], final_g, tq, tp)
        convs.append(conv_new)
        ssds.append(ssd_new.reshape(bsz, SSD_HEADS, SSD_HEADDIM, SSD_STATE))
        res.append(hr.reshape(bsz, S5_GROUPS, S5_STATE))
        ims.append(hi.reshape(bsz, S5_GROUPS, S5_STATE))
    return x, jnp.stack(convs), jnp.stack(ssds), jnp.stack(res), jnp.stack(ims)


def kernel(x_prompt, x_sample, state_ssd_conv, state_ssd, state_s5_re, state_s5_im, norm_g, w_in, conv_w, conv_b, dt_bias, a_log, d_ssd, ssd_norm_g, lam_re, lam_im, log_dt, b_re, b_im, c_re, c_im, d_s5, w_glu, b_glu, w_out, final_norm_g):
    layers = [_prep_layer(l, norm_g, w_in, conv_w, conv_b, dt_bias, a_log, d_ssd, ssd_norm_g,
                          lam_re, lam_im, log_dt, b_re, b_im, c_re, c_im, d_s5, w_glu, b_glu,
                          w_out) for l in range(DEPTH)]
    final_g = final_norm_g[None, :]
    e = (jnp.arange(DT_PAD)[:, None] == (jnp.arange(D_SSD) // SSD_HEADDIM)[None, :]).astype(BF16)
    expand = jnp.concatenate([e, e, e], axis=0)
    yp, conv_p, ssd_p, re_p, im_p = _trunk(x_prompt, None, layers, final_g, expand)
    ys, conv_s, ssd_s, re_s, im_s = _trunk(
        x_sample, (state_ssd_conv, state_ssd, state_s5_re, state_s5_im), layers, final_g, expand)
    return (yp, ys, conv_p, ssd_p, re_p, im_p, conv_s, ssd_s, re_s, im_s)
```

```python
import functools
import math

import jax
import jax.numpy as jnp
from jax import lax
from jax.experimental import pallas as pl
from jax.experimental.pallas import tpu as pltpu

D_MODEL = 1024
DEPTH = 4
D_SSD = 1024
D_S5 = 1024
SSD_HEADDIM = 64
SSD_HEADS = 16
SSD_GROUPS = 4
SSD_STATE = 128
SSD_CONV = 4
SSD_CONV_DIM = D_SSD + 2 * SSD_GROUPS * SSD_STATE
S5_GROUP_CH = 16
S5_GROUPS = 64
S5_STATE = 64
S5_LANES = S5_GROUPS * S5_STATE
EPS = 1e-6

LANE = 128
SUBLANE = 8
MXU_DIM = 256
DT_PAD = LANE
S5_KT = MXU_DIM
S5_NT = S5_KT // S5_GROUP_CH * S5_STATE
S5_NJ = D_S5 // S5_KT
S5_SLAB = 512
SSD_BG = 4
VMEM_LIMIT = 56 * 1024 * 1024

F32 = jnp.float32
BF16 = jnp.bfloat16


def _tiles(bsz, length):
    tp = MXU_DIM // bsz
    tq = max(tp, min(length, 512 // bsz))
    q = min(length, LANE)
    return dict(tq=tq, tp=tp, q=q, m=bsz * tq, bg=SSD_BG)


LOG2E = math.log2(math.e)


def _sigmoid(x):
    return 1.0 / (1.0 + jnp.exp2(x * -LOG2E))


def _silu(x):
    return x * _sigmoid(x)


def _softplus(x):
    return jnp.maximum(x, 0.0) + jnp.log1p(jnp.exp(-jnp.abs(x)))


def _gelu_tanh(x):
    c = math.sqrt(2.0 / math.pi)
    return 0.5 * x * (1.0 + jnp.tanh(c * (x + 0.044715 * (x * x * x))))


def _dot(a, b):
    return jnp.dot(a, b, preferred_element_type=F32)


def _rmsnorm(x, g):
    ms = jnp.mean(x * x, axis=-1, keepdims=True)
    return (x * lax.rsqrt(ms + EPS)) * g


def _in_rows(x, g_ref, p_ref, wu_ref, wzb_ref, h_ref, u_ref, zb_ref, bsz, tq, tp):
    h = _rmsnorm(x, g_ref[...]).astype(BF16)
    h3 = h.reshape(bsz, tq, D_MODEL)
    h_ref[...] = h3
    parts = [_dot(p_ref[...], h3[:, s * tp:(s + 1) * tp, :].reshape(bsz * tp, D_MODEL)).astype(BF16)
             for s in range(tq // tp)]
    h_t = jnp.concatenate(parts, axis=0)
    u_ref[...] = _dot(h_t, wu_ref[...])
    zb_ref[...] = _dot(h_t, wzb_ref[...])


def _inproj_kernel(x_ref, g_ref, p_ref, wu_ref, wzb_ref, h_ref, u_ref, zb_ref, *, bsz, tq, tp):
    x = x_ref[...].reshape(bsz * tq, D_MODEL)
    _in_rows(x, g_ref, p_ref, wu_ref, wzb_ref, h_ref, u_ref, zb_ref, bsz, tq, tp)


def _inproj(x, g, perm, wu, wzb, tq, tp):
    bsz, length, _ = x.shape
    m = bsz * tq
    full = lambda shape: pl.BlockSpec(shape, lambda i: (0,) * len(shape))
    tmaj = pl.BlockSpec((m, D_S5), lambda i: (i, 0))
    bmaj = pl.BlockSpec((bsz, tq, D_MODEL), lambda i: (0, i, 0))
    return pl.pallas_call(
        functools.partial(_inproj_kernel, bsz=bsz, tq=tq, tp=tp),
        grid=(length // tq,),
        in_specs=[bmaj, full((1, D_MODEL)), full(perm.shape), full(wu.shape), full(wzb.shape)],
        out_specs=[bmaj, tmaj, tmaj],
        out_shape=[jax.ShapeDtypeStruct(x.shape, BF16)]
                  + [jax.ShapeDtypeStruct((length * bsz, D_S5), F32)] * 2,
        compiler_params=pltpu.CompilerParams(
            dimension_semantics=("parallel",), vmem_limit_bytes=VMEM_LIMIT),
        name="inproj",
    )(x, g, perm, wu, wzb)


def _cumsum_rows(x):
    n = x.shape[0]
    row = lax.broadcasted_iota(jnp.int32, x.shape, 0)
    s = 1
    while s < n:
        x = x + jnp.where(row >= s, pltpu.roll(x, s, axis=0), 0.0)
        s *= 2
    return x


def _split3(x):
    b0 = x.astype(BF16)
    r = x - b0.astype(F32)
    b1 = r.astype(BF16)
    b2 = (r - b1.astype(F32)).astype(BF16)
    return b0, b1, b2


def _ssd_chunk(za_v, xbc_v, dt_v, prev_v, st_v, y_v, prm, q, after_group):
    cw_ref, cb_ref, dtb_ref, alog_ref, dexp_ref, ng_ref, expand_ref = prm
    k1 = SSD_CONV - 1
    x_cur = xbc_v[...]
    xx = jnp.concatenate([prev_v[...], x_cur], axis=0)
    conv = cb_ref[...] + cw_ref[k1:k1 + 1, :] * x_cur
    for s in range(1, SSD_CONV):
        conv = conv + cw_ref[k1 - s:k1 - s + 1, :] * pltpu.roll(xx, s, axis=0)[SUBLANE:]
    prev_v[...] = x_cur[q - SUBLANE:]

    act = _silu(conv)
    xs = act[:, :D_SSD]
    bm = act[:, D_SSD:D_SSD + SSD_GROUPS * SSD_STATE]
    cm = act[:, D_SSD + SSD_GROUPS * SSD_STATE:]

    dt = _softplus(dt_v[...] + dtb_ref[...])
    a = dt * (-jnp.exp(alog_ref[...]))
    a_cs = _cumsum_rows(a) * LOG2E
    if q % LANE == 0:
        a_cs_t = a_cs.T
    else:
        a_cs_t = jnp.concatenate([a_cs, jnp.zeros((LANE - q, LANE), F32)], axis=0).T[:, :q]

    both = jnp.concatenate(_split3(jnp.concatenate([dt, a_cs], axis=0)), axis=1)
    both_e = _dot(both, expand_ref[...])
    dt_e = both_e[:q]
    acs_e = both_e[q:]
    from_h = jnp.exp2(acs_e)
    to_end = jnp.exp2(acs_e[q - 1:q, :] - acs_e)
    xdt = xs * dt_e
    xend = (xdt * to_end).astype(BF16)
    xdt_b = xdt.astype(BF16)

    li = lax.broadcasted_iota(jnp.int32, (q, q), 0)
    si = lax.broadcasted_iota(jnp.int32, (q, q), 1)
    causal = li >= si
    hp = SSD_HEADS // SSD_GROUPS
    gw = hp * SSD_HEADDIM
    for g in range(SSD_GROUPS):
        sl = slice(g * gw, (g + 1) * gw)
        bg = bm[:, g * SSD_STATE:(g + 1) * SSD_STATE].astype(BF16)
        cg = cm[:, g * SSD_STATE:(g + 1) * SSD_STATE].astype(BF16)
        cbm = lax.dot_general(cg, bg, (((1,), (1,)), ((), ())), preferred_element_type=F32)
        st_g = st_v[:, sl]
        y_g = _dot(cg, st_g.astype(BF16)) * from_h[:, sl]
        parts = []
        for r in range(hp):
            h = g * hp + r
            seg = a_cs[:, h:h + 1] - a_cs_t[h:h + 1, :]
            decay = jnp.exp2(jnp.where(causal, seg, -jnp.inf))
            m = (cbm * decay).astype(BF16)
            parts.append(_dot(m, xdt_b[:, h * SSD_HEADDIM:(h + 1) * SSD_HEADDIM]))
        y_g = y_g + jnp.concatenate(parts, axis=1)
        upd = lax.dot_general(bg, xend[:, sl], (((0,), (0,)), ((), ())),
                              preferred_element_type=F32)
        st_v[:, sl] = st_g * from_h[q - 1:q, sl] + upd
        y_g = y_g + dexp_ref[:, sl] * xs[:, sl]
        y_g = y_g * _silu(za_v[:, sl])
        y_v[:, sl] = _rmsnorm(y_g, ng_ref[:, sl]).astype(y_v.dtype)
        after_group(g)


def _ssd_kernel(*refs, q, bg, n_windows, has_state):
    if has_state:
        h_ref, wza_ref, wxbc_ref, wdt_ref, convp_ref, h0_ref = refs[:6]
        refs = refs[6:]
    else:
        h_ref, wza_ref, wxbc_ref, wdt_ref = refs[:4]
        refs = refs[4:]
    prm = refs[:7]
    y_ref, convo_ref, ho_ref, za_s, xbc_s, dt_s, prev_ref, st_ref = refs[7:]
    c = pl.program_id(1)
    last = pl.num_programs(1) - 1
    k1 = SSD_CONV - 1

    pieces = [(dst, w_ref, c0, min(MXU_DIM, w_ref.shape[1] - c0))
              for dst, w_ref in ((xbc_s, wxbc_ref), (za_s, wza_ref), (dt_s, wdt_ref))
              for c0 in range(0, w_ref.shape[1], MXU_DIM)]

    def project(slot, k):
        dst, w_ref, c0, w = pieces[k]
        h = h_ref[...].reshape(bg * q, D_MODEL)
        dst[slot, :, :, c0:c0 + w] = _dot(h, w_ref[:, c0:c0 + w]).reshape(bg, q, w)

    def chunks(o, slot):
        per_group = -(-len(pieces) // (bg * SSD_GROUPS))
        for b in range(bg):
            def after_group(g, b=b):
                if slot is None:
                    return
                k0 = (b * SSD_GROUPS + g) * per_group
                for k in range(k0, min(k0 + per_group, len(pieces))):
                    project(slot, k)
            _ssd_chunk(za_s.at[o, b], xbc_s.at[o, b], dt_s.at[o, b], prev_ref.at[b],
                       st_ref.at[b], y_ref.at[b], prm, q, after_group)

    @pl.when(c == 0)
    def _():
        prev_ref[...] = jnp.zeros(prev_ref.shape, F32)
        if has_state:
            prev_ref[:, SUBLANE - k1:, :] = convp_ref[...]
            for b in range(bg):
                st_ref[b] = h0_ref[b].T
        else:
            st_ref[...] = jnp.zeros(st_ref.shape, F32)
        for k in range(len(pieces)):
            project(0, k)

    for slot in range(2):
        @pl.when((c > 0) & (c < last) & (c % 2 == slot))
        def _(slot=slot):
            chunks(1 - slot, slot)

    @pl.when(c == last)
    def _():
        chunks((n_windows - 1) % 2, None)
        convo_ref[...] = prev_ref[:, SUBLANE - k1:, :]
        for b in range(bg):
            ho_ref[b] = st_ref[b].T


def _ssd(h, wza, wxbc, wdt, conv_prev, h0, cw, cb, dtb, alog, dexp, ng, expand, q, bg):
    bsz, length, _ = h.shape
    nc = length // q
    has_state = conv_prev is not None
    full = lambda shape: pl.BlockSpec(shape, lambda b, c: (0,) * len(shape))
    per_b = lambda shape: pl.BlockSpec((bg,) + shape, lambda b, c: (b, 0, 0))
    k1 = SSD_CONV - 1
    in_specs = [pl.BlockSpec((bg, q, D_MODEL), lambda b, c: (b, jnp.minimum(c, nc - 1), 0)),
                full(wza.shape), full(wxbc.shape), full(wdt.shape)]
    args = [h, wza, wxbc, wdt]
    if has_state:
        in_specs += [per_b((k1, SSD_CONV_DIM)), per_b((D_SSD, SSD_STATE))]
        args += [conv_prev, h0]
    params = [cw, cb, dtb, alog, dexp, ng, expand]
    in_specs += [full(p.shape) for p in params]
    args += params
    return pl.pallas_call(
        functools.partial(_ssd_kernel, q=q, bg=bg, n_windows=nc, has_state=has_state),
        grid=(bsz // bg, nc + 1),
        in_specs=in_specs,
        out_specs=[pl.BlockSpec((bg, q, D_SSD), lambda b, c: (b, jnp.maximum(c - 1, 0), 0)),
                   per_b((k1, SSD_CONV_DIM)), per_b((D_SSD, SSD_STATE))],
        out_shape=[jax.ShapeDtypeStruct((bsz, length, D_SSD), BF16),
                   jax.ShapeDtypeStruct((bsz, k1, SSD_CONV_DIM), F32),
                   jax.ShapeDtypeStruct((bsz, D_SSD, SSD_STATE), F32)],
        scratch_shapes=[pltpu.VMEM((2, bg, q, D_SSD), F32),
                        pltpu.VMEM((2, bg, q, SSD_CONV_DIM), F32),
                        pltpu.VMEM((2, bg, q, DT_PAD), F32),
                        pltpu.VMEM((bg, SUBLANE, SSD_CONV_DIM), F32),
                        pltpu.VMEM((bg, SSD_STATE, D_SSD), F32)],
        compiler_params=pltpu.CompilerParams(
            dimension_semantics=("parallel", "arbitrary"), vmem_limit_bytes=VMEM_LIMIT),
        name="ssd",
    )(*args)


def _s5_kernel(*refs, bsz, tq, has_state):
    if has_state:
        (u_ref, zb_ref, h0r_ref, h0i_ref, lr_ref, li_ref, bre_ref, bim_ref, cre_ref, cim_ref,
         d_ref, wg_ref, bg_ref, y_ref, hro_ref, hio_ref, bur_ref, bui_ref, hr_ref, hi_ref) = refs
    else:
        (u_ref, zb_ref, lr_ref, li_ref, bre_ref, bim_ref, cre_ref, cim_ref,
         d_ref, wg_ref, bg_ref, y_ref, hro_ref, hio_ref, bur_ref, bui_ref, hr_ref, hi_ref) = refs
    i = pl.program_id(0)
    last = pl.num_programs(0) - 1

    @pl.when(i == 0)
    def _():
        if has_state:
            hr_ref[...] = h0r_ref[...]
            hi_ref[...] = h0i_ref[...]
        else:
            hr_ref[...] = jnp.zeros(hr_ref.shape, F32)
            hi_ref[...] = jnp.zeros(hi_ref.shape, F32)

    u = u_ref[...]
    u_b = u.astype(BF16)
    n_slab = S5_NT // S5_SLAB
    units = [(j, s) for j in range(S5_NJ) for s in range(n_slab)]

    def project_in(k):
        j, s = units[k]
        uj = u_b[:, j * S5_KT:(j + 1) * S5_KT]
        cols = slice(s * S5_SLAB, (s + 1) * S5_SLAB)
        bur_ref[k % 2] = _dot(uj, bre_ref[j, :, cols])
        bui_ref[k % 2] = _dot(uj, bim_ref[j, :, cols])

    y_parts = [None] * S5_NJ
    gate_parts = [None] * S5_NJ
    project_in(0)
    for k, (j, s) in enumerate(units):
        if k + 1 < len(units):
            project_in(k + 1)
        p = k % 2
        lo = j * S5_NT + s * S5_SLAB
        lam_r = jnp.broadcast_to(lr_ref[:, lo:lo + S5_SLAB], (bsz, S5_SLAB))
        lam_i = jnp.broadcast_to(li_ref[:, lo:lo + S5_SLAB], (bsz, S5_SLAB))
        hr = hr_ref[:, lo:lo + S5_SLAB]
        hi = hi_ref[:, lo:lo + S5_SLAB]
        for t in range(tq):
            rows = slice(t * bsz, (t + 1) * bsz)
            nr = lam_r * hr - lam_i * hi + bur_ref[p, rows, :]
            ni = lam_r * hi + lam_i * hr + bui_ref[p, rows, :]
            bur_ref[p, rows, :] = nr
            bui_ref[p, rows, :] = ni
            hr, hi = nr, ni
        hr_ref[:, lo:lo + S5_SLAB] = hr
        hi_ref[:, lo:lo + S5_SLAB] = hi
        crow = slice(s * S5_SLAB, (s + 1) * S5_SLAB)
        yk = (_dot(bur_ref[p].astype(BF16), cre_ref[j, crow, :])
              + _dot(bui_ref[p].astype(BF16), cim_ref[j, crow, :]))
        y_parts[j] = yk if y_parts[j] is None else y_parts[j] + yk
        if s == n_slab - 1:
            jc = slice(j * S5_KT, (j + 1) * S5_KT)
            y_parts[j] = _gelu_tanh(y_parts[j] + d_ref[:, jc] * u[:, jc])
            gate_parts[j] = y_parts[j] * _silu(zb_ref[:, jc])
    y_b = jnp.concatenate(y_parts, axis=1).astype(BF16)
    for n in range(S5_NJ):
        nc = slice(n * S5_KT, (n + 1) * S5_KT)
        z = _dot(y_b, wg_ref[:, nc]) + bg_ref[:, nc]
        y_ref[:, nc] = (gate_parts[n] * _sigmoid(z)).astype(y_ref.dtype)

    @pl.when(i == last)
    def _():
        hro_ref[...] = hr_ref[...]
        hio_ref[...] = hi_ref[...]


def _s5(u, zb, h0r, h0i, lr, li, bre, bim, cre, cim, d, wg, bg, bsz, m):
    rows = u.shape[0]
    has_state = h0r is not None
    tq = m // bsz
    blk = pl.BlockSpec((m, D_S5), lambda i: (i, 0))
    full = lambda shape: pl.BlockSpec(shape, lambda i: (0,) * len(shape))
    in_specs = [blk, blk]
    args = [u, zb]
    if has_state:
        in_specs += [full(h0r.shape), full(h0i.shape)]
        args += [h0r, h0i]
    params = [lr, li, bre, bim, cre, cim, d, wg, bg]
    in_specs += [full(p.shape) for p in params]
    args += params
    st = jax.ShapeDtypeStruct((bsz, S5_LANES), F32)
    return pl.pallas_call(
        functools.partial(_s5_kernel, bsz=bsz, tq=tq, has_state=has_state),
        grid=(rows // m,),
        in_specs=in_specs,
        out_specs=[blk, full((bsz, S5_LANES)), full((bsz, S5_LANES))],
        out_shape=[jax.ShapeDtypeStruct((rows, D_S5), BF16), st, st],
        scratch_shapes=[pltpu.VMEM((2, m, S5_SLAB), F32), pltpu.VMEM((2, m, S5_SLAB), F32),
                        pltpu.VMEM((bsz, S5_LANES), F32), pltpu.VMEM((bsz, S5_LANES), F32)],
        compiler_params=pltpu.CompilerParams(
            dimension_semantics=("arbitrary",), vmem_limit_bytes=VMEM_LIMIT),
        name="s5",
    )(*args)


def _out_rows(x_ref, ya_ref, yb_ref, pt_ref, w_ref, bsz, tq, tp):
    m = bsz * tq
    mp = bsz * tp
    yb = yb_ref[...]
    parts = [_dot(pt_ref[...], yb[s * mp:(s + 1) * mp, :]).astype(BF16).reshape(bsz, tp, D_S5)
             for s in range(tq // tp)]
    yb_b = jnp.concatenate(parts, axis=1).reshape(m, D_S5)
    ya = ya_ref[...].reshape(m, D_SSD)
    return x_ref[...].reshape(m, D_MODEL) + _dot(jnp.concatenate([ya, yb_b], axis=1), w_ref[...])


def _outproj_kernel(x_ref, ya_ref, yb_ref, pt_ref, w_ref, g_ref, o_ref, *, bsz, tq, tp):
    x = _out_rows(x_ref, ya_ref, yb_ref, pt_ref, w_ref, bsz, tq, tp)
    o_ref[...] = _rmsnorm(x, g_ref[...]).reshape(bsz, tq, D_MODEL)


def _outin_kernel(x_ref, ya_ref, yb_ref, pt_ref, w_ref, g_ref, p_ref, wu_ref, wzb_ref,
                  o_ref, h_ref, u_ref, zb_ref, *, bsz, tq, tp):
    x = _out_rows(x_ref, ya_ref, yb_ref, pt_ref, w_ref, bsz, tq, tp)
    o_ref[...] = x.reshape(bsz, tq, D_MODEL)
    _in_rows(x, g_ref, p_ref, wu_ref, wzb_ref, h_ref, u_ref, zb_ref, bsz, tq, tp)


def _outproj(x, ya, yb, perm_t, w, g, tq, tp, nxt=None):
    bsz, length, _ = x.shape
    m = bsz * tq
    bmaj = pl.BlockSpec((bsz, tq, D_MODEL), lambda i: (0, i, 0))
    tmaj = pl.BlockSpec((m, D_S5), lambda i: (i, 0))
    full = lambda shape: pl.BlockSpec(shape, lambda i: (0,) * len(shape))
    args = [x, ya, yb, perm_t, w, g]
    in_specs = [bmaj, bmaj, tmaj, full(perm_t.shape), full(w.shape), full(g.shape)]
    x_out = jax.ShapeDtypeStruct(x.shape, F32)
    if nxt is None:
        body, out_specs, out_shape = _outproj_kernel, bmaj, x_out
    else:
        args += list(nxt)
        in_specs += [full(a.shape) for a in nxt]
        body = _outin_kernel
        out_specs = [bmaj, bmaj, tmaj, tmaj]
        out_shape = [x_out, jax.ShapeDtypeStruct(x.shape, BF16)] \
            + [jax.ShapeDtypeStruct((length * bsz, D_S5), F32)] * 2
    return pl.pallas_call(
        functools.partial(body, bsz=bsz, tq=tq, tp=tp),
        grid=(length // tq,),
        in_specs=in_specs,
        out_specs=out_specs,
        out_shape=out_shape,
        compiler_params=pltpu.CompilerParams(
            dimension_semantics=("parallel",), vmem_limit_bytes=VMEM_LIMIT),
        name="outproj",
    )(*args)


def _prep_layer(l, norm_g, w_in, conv_w, conv_b, dt_bias, a_log, d_ssd, ssd_norm_g,
                lam_re, lam_im, log_dt, b_re, b_im, c_re, c_im, d_s5, w_glu, b_glu, w_out):
    w = w_in[l]
    o1 = D_SSD
    o2 = o1 + SSD_CONV_DIM
    o3 = o2 + SSD_HEADS
    o4 = o3 + D_S5
    wdt = jnp.zeros((D_MODEL, DT_PAD), F32).at[:, :SSD_HEADS].set(w[:, o2:o3])
    pad_h = lambda v: jnp.zeros((1, DT_PAD), F32).at[0, :SSD_HEADS].set(v)
    lam = lax.complex(lam_re[l], lam_im[l])
    lam_bar = jnp.exp(lam * jnp.exp(log_dt[l])[:, None])
    b_bar = ((lam_bar - 1.0) / lam)[..., None] * lax.complex(b_re[l], b_im[l])
    gpt = S5_KT // S5_GROUP_CH
    same_group = ((jnp.arange(S5_KT) // S5_GROUP_CH)[:, None]
                  == (jnp.arange(S5_NT) // S5_STATE)[None, :])

    def blockdiag_b(v):
        v = jnp.transpose(v, (0, 2, 1)).reshape(S5_NJ, S5_KT, S5_STATE)
        return jnp.where(same_group, jnp.tile(v, (1, 1, gpt)), 0.0).astype(BF16)

    def blockdiag_c(v):
        v = jnp.transpose(v, (0, 2, 1)).reshape(S5_NJ, S5_NT, S5_GROUP_CH)
        return jnp.where(same_group.T, jnp.tile(v, (1, 1, gpt)), 0.0).astype(BF16)

    return dict(
        norm_g=norm_g[l][None, :],
        wza=w[:, :o1].astype(BF16), wxbc=w[:, o1:o2].astype(BF16), wdt=wdt.astype(BF16),
        wu=w[:, o3:o4].astype(BF16), wzb=w[:, o4:].astype(BF16),
        cw=conv_w[l], cb=conv_b[l][None, :], dtb=pad_h(dt_bias[l]), alog=pad_h(a_log[l]),
        dexp=jnp.repeat(d_ssd[l], SSD_HEADDIM)[None, :], ng=ssd_norm_g[l][None, :],
        lr=jnp.real(lam_bar).reshape(1, S5_LANES), li=jnp.imag(lam_bar).reshape(1, S5_LANES),
        bre=blockdiag_b(jnp.real(b_bar)), bim=blockdiag_b(jnp.imag(b_bar)),
        cre=blockdiag_c(c_re[l]), cim=blockdiag_c(-c_im[l]),
        d5=d_s5[l][None, :], wg=w_glu[l].astype(BF16), bg=b_glu[l][None, :],
        wout=w_out[l].astype(BF16),
    )


def _trunk(x, states, layers, final_g, expand):
    bsz, length, _ = x.shape
    tl = _tiles(bsz, length)
    tq, tp = tl['tq'], tl['tp']
    r = jnp.arange(bsz * tp)
    perm = (((r % bsz) * tp + r // bsz)[:, None] == r[None, :]).astype(BF16)
    convs, ssds, res, ims = [], [], [], []
    h, u, zb = _inproj(x, layers[0]['norm_g'], perm, layers[0]['wu'], layers[0]['wzb'], tq, tp)
    for l, p in enumerate(layers):
        if states is None:
            conv_prev = h0 = h0r = h0i = None
        else:
            conv_s, ssd_s, s5re_s, s5im_s = states
            conv_prev = conv_s[l]
            h0 = ssd_s[l].reshape(bsz, D_SSD, SSD_STATE)
            h0r = s5re_s[l].reshape(bsz, S5_LANES)
            h0i = s5im_s[l].reshape(bsz, S5_LANES)
        ya, conv_new, ssd_new = _ssd(h, p['wza'], p['wxbc'], p['wdt'], conv_prev, h0,
                                     p['cw'], p['cb'], p['dtb'], p['alog'], p['dexp'], p['ng'],
                                     expand, tl['q'], tl['bg'])
        yb, hr, hi = _s5(u, zb, h0r, h0i, p['lr'], p['li'], p['bre'], p['bim'],
                         p['cre'], p['cim'], p['d5'], p['wg'], p['bg'], bsz, tl['m'])
        if l + 1 < len(layers):
            n = layers[l + 1]
            x, h, u, zb = _outproj(x, ya, yb, perm.T, p['wout'], n['norm_g'], tq, tp,
                                   nxt=(perm, n['wu'], n['wzb']))
        else:
            x = _outproj(x, ya, yb, perm.T, p['wout'], final_g, tq, tp)
        convs.append(conv_new)
        ssds.append(ssd_new.reshape(bsz, SSD_HEADS, SSD_HEADDIM, SSD_STATE))
        res.append(hr.reshape(bsz, S5_GROUPS, S5_STATE))
        ims.append(hi.reshape(bsz, S5_GROUPS, S5_STATE))
    return x, jnp.stack(convs), jnp.stack(ssds), jnp.stack(res), jnp.stack(ims)


def kernel(x_prompt, x_sample, state_ssd_conv, state_ssd, state_s5_re, state_s5_im, norm_g, w_in, conv_w, conv_b, dt_bias, a_log, d_ssd, ssd_norm_g, lam_re, lam_im, log_dt, b_re, b_im, c_re, c_im, d_s5, w_glu, b_glu, w_out, final_norm_g):
    layers = [_prep_layer(l, norm_g, w_in, conv_w, conv_b, dt_bias, a_log, d_ssd, ssd_norm_g,
                          lam_re, lam_im, log_dt, b_re, b_im, c_re, c_im, d_s5, w_glu, b_glu,
                          w_out) for l in range(DEPTH)]
    final_g = final_norm_g[None, :]
    e = (jnp.arange(DT_PAD)[:, None] == (jnp.arange(D_SSD) // SSD_HEADDIM)[None, :]).astype(BF16)
    expand = jnp.concatenate([e, e, e], axis=0)
    yp, conv_p, ssd_p, re_p, im_p = _trunk(x_prompt, None, layers, final_g, expand)
    ys, conv_s, ssd_s, re_s, im_s = _trunk(
        x_sample, (state_ssd_conv, state_ssd, state_s5_re, state_s5_im), layers, final_g, expand)
    return (yp, ys, conv_p, ssd_p, re_p, im_p, conv_s, ssd_s, re_s, im_s)
```

```python
import functools
import math

import jax
import jax.numpy as jnp
from jax import lax
from jax.experimental import pallas as pl
from jax.experimental.pallas import tpu as pltpu

D_MODEL = 1024
DEPTH = 4
D_SSD = 1024
D_S5 = 1024
SSD_HEADDIM = 64
SSD_HEADS = 16
SSD_GROUPS = 4
SSD_STATE = 128
SSD_CONV = 4
SSD_CONV_DIM = D_SSD + 2 * SSD_GROUPS * SSD_STATE
S5_GROUP_CH = 16
S5_GROUPS = 64
S5_STATE = 64
S5_LANES = S5_GROUPS * S5_STATE
EPS = 1e-6

LANE = 128
SUBLANE = 8
MXU_DIM = 256
DT_PAD = LANE
S5_KT = MXU_DIM
S5_NT = S5_KT // S5_GROUP_CH * S5_STATE
S5_NJ = D_S5 // S5_KT
S5_SLAB = 512
SSD_BG = 4
VMEM_LIMIT = 56 * 1024 * 1024

F32 = jnp.float32
BF16 = jnp.bfloat16


def _tiles(bsz, length):
    tp = MXU_DIM // bsz
    tq = max(tp, min(length, 512 // bsz))
    q = min(length, LANE)
    return dict(tq=tq, tp=tp, q=q, m=bsz * tq, bg=SSD_BG)


LOG2E = math.log2(math.e)


def _sigmoid(x):
    return 1.0 / (1.0 + jnp.exp2(x * -LOG2E))


def _silu(x):
    return x * _sigmoid(x)


def _softplus(x):
    return jnp.maximum(x, 0.0) + jnp.log1p(jnp.exp(-jnp.abs(x)))


def _gelu_tanh(x):
    c = math.sqrt(2.0 / math.pi)
    return 0.5 * x * (1.0 + jnp.tanh(c * (x + 0.044715 * (x * x * x))))


def _dot(a, b):
    return jnp.dot(a, b, preferred_element_type=F32)


def _rmsnorm(x, g):
    ms = jnp.mean(x * x, axis=-1, keepdims=True)
    return (x * lax.rsqrt(ms + EPS)) * g


def _in_rows(x, g_ref, p_ref, wu_ref, wzb_ref, h_ref, u_ref, zb_ref, bsz, tq, tp):
    h = _rmsnorm(x, g_ref[...]).astype(BF16)
    h3 = h.reshape(bsz, tq, D_MODEL)
    h_ref[...] = h3
    parts = [_dot(p_ref[...], h3[:, s * tp:(s + 1) * tp, :].reshape(bsz * tp, D_MODEL)).astype(BF16)
             for s in range(tq // tp)]
    h_t = jnp.concatenate(parts, axis=0)
    u_ref[...] = _dot(h_t, wu_ref[...])
    zb_ref[...] = _dot(h_t, wzb_ref[...])


def _inproj_kernel(x_ref, g_ref, p_ref, wu_ref, wzb_ref, h_ref, u_ref, zb_ref, *, bsz, tq, tp):
    x = x_ref[...].reshape(bsz * tq, D_MODEL)
    _in_rows(x, g_ref, p_ref, wu_ref, wzb_ref, h_ref, u_ref, zb_ref, bsz, tq, tp)


def _inproj(x, g, perm, wu, wzb, tq, tp):
    bsz, length, _ = x.shape
    m = bsz * tq
    full = lambda shape: pl.BlockSpec(shape, lambda i: (0,) * len(shape))
    tmaj = pl.BlockSpec((m, D_S5), lambda i: (i, 0))
    bmaj = pl.BlockSpec((bsz, tq, D_MODEL), lambda i: (0, i, 0))
    return pl.pallas_call(
        functools.partial(_inproj_kernel, bsz=bsz, tq=tq, tp=tp),
        grid=(length // tq,),
        in_specs=[bmaj, full((1, D_MODEL)), full(perm.shape), full(wu.shape), full(wzb.shape)],
        out_specs=[bmaj, tmaj, tmaj],
        out_shape=[jax.ShapeDtypeStruct(x.shape, BF16)]
                  + [jax.ShapeDtypeStruct((length * bsz, D_S5), F32)] * 2,
        compiler_params=pltpu.CompilerParams(
            dimension_semantics=("parallel",), vmem_limit_bytes=VMEM_LIMIT),
        name="inproj",
    )(x, g, perm, wu, wzb)


def _cumsum_rows(x):
    n = x.shape[0]
    row = lax.broadcasted_iota(jnp.int32, x.shape, 0)
    s = 1
    while s < n:
        x = x + jnp.where(row >= s, pltpu.roll(x, s, axis=0), 0.0)
        s *= 2
    return x


def _split3(x):
    b0 = x.astype(BF16)
    r = x - b0.astype(F32)
    b1 = r.astype(BF16)
    b2 = (r - b1.astype(F32)).astype(BF16)
    return b0, b1, b2


def _ssd_chunk(za_v, xbc_v, dt_v, prev_v, st_v, y_v, prm, q, after_group):
    cw_ref, cb_ref, dtb_ref, alog_ref, dexp_ref, ng_ref, expand_ref = prm
    k1 = SSD_CONV - 1
    x_cur = xbc_v[...]
    xx = jnp.concatenate([prev_v[...], x_cur], axis=0)
    conv = cb_ref[...] + cw_ref[k1:k1 + 1, :] * x_cur
    for s in range(1, SSD_CONV):
        conv = conv + cw_ref[k1 - s:k1 - s + 1, :] * pltpu.roll(xx, s, axis=0)[SUBLANE:]
    prev_v[...] = x_cur[q - SUBLANE:]

    act = _silu(conv)
    xs = act[:, :D_SSD]
    bm = act[:, D_SSD:D_SSD + SSD_GROUPS * SSD_STATE]
    cm = act[:, D_SSD + SSD_GROUPS * SSD_STATE:]

    dt = _softplus(dt_v[...] + dtb_ref[...])
    a = dt * (-jnp.exp(alog_ref[...]))
    a_cs = _cumsum_rows(a) * LOG2E
    if q % LANE == 0:
        a_cs_t = a_cs.T
    else:
        a_cs_t = jnp.concatenate([a_cs, jnp.zeros((LANE - q, LANE), F32)], axis=0).T[:, :q]

    both = jnp.concatenate(_split3(jnp.concatenate([dt, a_cs], axis=0)), axis=1)
    both_e = _dot(both, expand_ref[...])
    dt_e = both_e[:q]
    acs_e = both_e[q:]
    from_h = jnp.exp2(acs_e)
    to_end = jnp.exp2(acs_e[q - 1:q, :] - acs_e)
    xdt = xs * dt_e
    xend = (xdt * to_end).astype(BF16)
    xdt_b = xdt.astype(BF16)

    li = lax.broadcasted_iota(jnp.int32, (q, q), 0)
    si = lax.broadcasted_iota(jnp.int32, (q, q), 1)
    causal = li >= si
    hp = SSD_HEADS // SSD_GROUPS
    gw = hp * SSD_HEADDIM
    for g in range(SSD_GROUPS):
        sl = slice(g * gw, (g + 1) * gw)
        bg = bm[:, g * SSD_STATE:(g + 1) * SSD_STATE].astype(BF16)
        cg = cm[:, g * SSD_STATE:(g + 1) * SSD_STATE].astype(BF16)
        cbm = lax.dot_general(cg, bg, (((1,), (1,)), ((), ())), preferred_element_type=F32)
        st_g = st_v[:, sl]
        y_g = _dot(cg, st_g.astype(BF16)) * from_h[:, sl]
        parts = []
        for r in range(hp):
            h = g * hp + r
            seg = a_cs[:, h:h + 1] - a_cs_t[h:h + 1, :]
            decay = jnp.exp2(jnp.where(causal, seg, -jnp.inf))
            m = (cbm * decay).astype(BF16)
            parts.append(_dot(m, xdt_b[:, h * SSD_HEADDIM:(h + 1) * SSD_HEADDIM]))
        y_g = y_g + jnp.concatenate(parts, axis=1)
        upd = lax.dot_general(bg, xend[:, sl], (((0,), (0,)), ((), ())),
                              preferred_element_type=F32)
        st_v[:, sl] = st_g * from_h[q - 1:q, sl] + upd
        y_g = y_g + dexp_ref[:, sl] * xs[:, sl]
        y_g = y_g * _silu(za_v[:, sl])
        y_v[:, sl] = _rmsnorm(y_g, ng_ref[:, sl]).astype(y_v.dtype)
        after_group(g)


def _ssd_kernel(*refs, q, bg, n_windows, has_state, has_acc):
    if has_state:
        h_ref, wza_ref, wxbc_ref, wdt_ref, convp_ref, h0_ref = refs[:6]
        refs = refs[6:]
    else:
        h_ref, wza_ref, wxbc_ref, wdt_ref = refs[:4]
        refs = refs[4:]
    prm = refs[:7]
    refs = refs[8:] if has_acc else refs[7:]
    y_ref, convo_ref, ho_ref, za_s, xbc_s, dt_s, prev_ref, st_ref = refs
    c = pl.program_id(1)
    last = pl.num_programs(1) - 1
    k1 = SSD_CONV - 1

    pieces = [(dst, w_ref, c0, min(MXU_DIM, w_ref.shape[1] - c0))
              for dst, w_ref in ((xbc_s, wxbc_ref), (za_s, wza_ref), (dt_s, wdt_ref))
              for c0 in range(0, w_ref.shape[1], MXU_DIM)]

    def project(slot, k):
        dst, w_ref, c0, w = pieces[k]
        h = h_ref[...].reshape(bg * q, D_MODEL)
        dst[slot, :, :, c0:c0 + w] = _dot(h, w_ref[:, c0:c0 + w]).reshape(bg, q, w)

    def chunks(o, slot):
        per_group = -(-len(pieces) // (bg * SSD_GROUPS))
        for b in range(bg):
            def after_group(g, b=b):
                if slot is None:
                    return
                k0 = (b * SSD_GROUPS + g) * per_group
                for k in range(k0, min(k0 + per_group, len(pieces))):
                    project(slot, k)
            _ssd_chunk(za_s.at[o, b], xbc_s.at[o, b], dt_s.at[o, b], prev_ref.at[b],
                       st_ref.at[b], y_ref.at[b], prm, q, after_group)

    @pl.when(c == 0)
    def _():
        prev_ref[...] = jnp.zeros(prev_ref.shape, F32)
        if has_state:
            prev_ref[:, SUBLANE - k1:, :] = convp_ref[...]
            for b in range(bg):
                st_ref[b] = h0_ref[b].T
        else:
            st_ref[...] = jnp.zeros(st_ref.shape, F32)
        for k in range(len(pieces)):
            project(0, k)

    for slot in range(2):
        @pl.when((c > 0) & (c < last) & (c % 2 == slot))
        def _(slot=slot):
            chunks(1 - slot, slot)

    @pl.when(c == last)
    def _():
        chunks((n_windows - 1) % 2, None)
        convo_ref[...] = prev_ref[:, SUBLANE - k1:, :]
        for b in range(bg):
            ho_ref[b] = st_ref[b].T


def _ssd(h, wza, wxbc, wdt, conv_prev, h0, cw, cb, dtb, alog, dexp, ng, expand, q, bg,
         layer, acc):
    bsz, length, _ = h.shape
    nc = length // q
    has_state = conv_prev is not None
    full = lambda shape: pl.BlockSpec(shape, lambda b, c: (0,) * len(shape))
    per_b = lambda shape: pl.BlockSpec((bg,) + shape, lambda b, c: (b, 0, 0))
    k1 = SSD_CONV - 1
    in_specs = [pl.BlockSpec((bg, q, D_MODEL), lambda b, c: (b, jnp.minimum(c, nc - 1), 0)),
                full(wza.shape), full(wxbc.shape), full(wdt.shape)]
    args = [h, wza, wxbc, wdt]
    if has_state:
        in_specs += [per_b((k1, SSD_CONV_DIM)), per_b((D_SSD, SSD_STATE))]
        args += [conv_prev, h0]
    params = [cw, cb, dtb, alog, dexp, ng, expand]
    in_specs += [full(p.shape) for p in params]
    args += params
    aliases = {}
    if acc is not None:
        aliases = {len(args): 2}
        in_specs.append(pl.BlockSpec(memory_space=pl.ANY))
        args.append(acc)
    return pl.pallas_call(
        functools.partial(_ssd_kernel, q=q, bg=bg, n_windows=nc, has_state=has_state,
                          has_acc=acc is not None),
        grid=(bsz // bg, nc + 1),
        in_specs=in_specs,
        out_specs=[pl.BlockSpec((bg, q, D_SSD), lambda b, c: (b, jnp.maximum(c - 1, 0), 0)),
                   per_b((k1, SSD_CONV_DIM)),
                   pl.BlockSpec((None, bg, D_SSD, SSD_STATE), lambda b, c: (layer, b, 0, 0))],
        out_shape=[jax.ShapeDtypeStruct((bsz, length, D_SSD), BF16),
                   jax.ShapeDtypeStruct((bsz, k1, SSD_CONV_DIM), F32),
                   jax.ShapeDtypeStruct((DEPTH, bsz, D_SSD, SSD_STATE), F32)],
        input_output_aliases=aliases,
        scratch_shapes=[pltpu.VMEM((2, bg, q, D_SSD), F32),
                        pltpu.VMEM((2, bg, q, SSD_CONV_DIM), F32),
                        pltpu.VMEM((2, bg, q, DT_PAD), F32),
                        pltpu.VMEM((bg, SUBLANE, SSD_CONV_DIM), F32),
                        pltpu.VMEM((bg, SSD_STATE, D_SSD), F32)],
        compiler_params=pltpu.CompilerParams(
            dimension_semantics=("parallel", "arbitrary"), vmem_limit_bytes=VMEM_LIMIT),
        name="ssd",
    )(*args)


def _s5_kernel(*refs, bsz, tq, has_state):
    if has_state:
        (u_ref, zb_ref, h0r_ref, h0i_ref, lr_ref, li_ref, bre_ref, bim_ref, cre_ref, cim_ref,
         d_ref, wg_ref, bg_ref, y_ref, hro_ref, hio_ref, bur_ref, bui_ref, hr_ref, hi_ref) = refs
    else:
        (u_ref, zb_ref, lr_ref, li_ref, bre_ref, bim_ref, cre_ref, cim_ref,
         d_ref, wg_ref, bg_ref, y_ref, hro_ref, hio_ref, bur_ref, bui_ref, hr_ref, hi_ref) = refs
    i = pl.program_id(0)
    last = pl.num_programs(0) - 1

    @pl.when(i == 0)
    def _():
        if has_state:
            hr_ref[...] = h0r_ref[...]
            hi_ref[...] = h0i_ref[...]
        else:
            hr_ref[...] = jnp.zeros(hr_ref.shape, F32)
            hi_ref[...] = jnp.zeros(hi_ref.shape, F32)

    u = u_ref[...]
    u_b = u.astype(BF16)
    n_slab = S5_NT // S5_SLAB
    units = [(j, s) for j in range(S5_NJ) for s in range(n_slab)]

    def project_in(k):
        j, s = units[k]
        uj = u_b[:, j * S5_KT:(j + 1) * S5_KT]
        cols = slice(s * S5_SLAB, (s + 1) * S5_SLAB)
        bur_ref[k % 2] = _dot(uj, bre_ref[j, :, cols])
        bui_ref[k % 2] = _dot(uj, bim_ref[j, :, cols])

    y_parts = [None] * S5_NJ
    gate_parts = [None] * S5_NJ
    project_in(0)
    for k, (j, s) in enumerate(units):
        if k + 1 < len(units):
            project_in(k + 1)
        p = k % 2
        lo = j * S5_NT + s * S5_SLAB
        lam_r = jnp.broadcast_to(lr_ref[:, lo:lo + S5_SLAB], (bsz, S5_SLAB))
        lam_i = jnp.broadcast_to(li_ref[:, lo:lo + S5_SLAB], (bsz, S5_SLAB))
        hr = hr_ref[:, lo:lo + S5_SLAB]
        hi = hi_ref[:, lo:lo + S5_SLAB]
        for t in range(tq):
            rows = slice(t * bsz, (t + 1) * bsz)
            nr = lam_r * hr - lam_i * hi + bur_ref[p, rows, :]
            ni = lam_r * hi + lam_i * hr + bui_ref[p, rows, :]
            bur_ref[p, rows, :] = nr
            bui_ref[p, rows, :] = ni
            hr, hi = nr, ni
        hr_ref[:, lo:lo + S5_SLAB] = hr
        hi_ref[:, lo:lo + S5_SLAB] = hi
        crow = slice(s * S5_SLAB, (s + 1) * S5_SLAB)
        yk = (_dot(bur_ref[p].astype(BF16), cre_ref[j, crow, :])
              + _dot(bui_ref[p].astype(BF16), cim_ref[j, crow, :]))
        y_parts[j] = yk if y_parts[j] is None else y_parts[j] + yk
        if s == n_slab - 1:
            jc = slice(j * S5_KT, (j + 1) * S5_KT)
            y_parts[j] = _gelu_tanh(y_parts[j] + d_ref[:, jc] * u[:, jc])
            gate_parts[j] = y_parts[j] * _silu(zb_ref[:, jc])
    y_b = jnp.concatenate(y_parts, axis=1).astype(BF16)
    for n in range(S5_NJ):
        nc = slice(n * S5_KT, (n + 1) * S5_KT)
        z = _dot(y_b, wg_ref[:, nc]) + bg_ref[:, nc]
        y_ref[:, nc] = (gate_parts[n] * _sigmoid(z)).astype(y_ref.dtype)

    @pl.when(i == last)
    def _():
        hro_ref[...] = hr_ref[...]
        hio_ref[...] = hi_ref[...]


def _s5(u, zb, h0r, h0i, lr, li, bre, bim, cre, cim, d, wg, bg, bsz, m):
    rows = u.shape[0]
    has_state = h0r is not None
    tq = m // bsz
    blk = pl.BlockSpec((m, D_S5), lambda i: (i, 0))
    full = lambda shape: pl.BlockSpec(shape, lambda i: (0,) * len(shape))
    in_specs = [blk, blk]
    args = [u, zb]
    if has_state:
        in_specs += [full(h0r.shape), full(h0i.shape)]
        args += [h0r, h0i]
    params = [lr, li, bre, bim, cre, cim, d, wg, bg]
    in_specs += [full(p.shape) for p in params]
    args += params
    st = jax.ShapeDtypeStruct((bsz, S5_LANES), F32)
    return pl.pallas_call(
        functools.partial(_s5_kernel, bsz=bsz, tq=tq, has_state=has_state),
        grid=(rows // m,),
        in_specs=in_specs,
        out_specs=[blk, full((bsz, S5_LANES)), full((bsz, S5_LANES))],
        out_shape=[jax.ShapeDtypeStruct((rows, D_S5), BF16), st, st],
        scratch_shapes=[pltpu.VMEM((2, m, S5_SLAB), F32), pltpu.VMEM((2, m, S5_SLAB), F32),
                        pltpu.VMEM((bsz, S5_LANES), F32), pltpu.VMEM((bsz, S5_LANES), F32)],
        compiler_params=pltpu.CompilerParams(
            dimension_semantics=("arbitrary",), vmem_limit_bytes=VMEM_LIMIT),
        name="s5",
    )(*args)


def _out_rows(x_ref, ya_ref, yb_ref, pt_ref, w_ref, bsz, tq, tp):
    m = bsz * tq
    mp = bsz * tp
    yb = yb_ref[...]
    parts = [_dot(pt_ref[...], yb[s * mp:(s + 1) * mp, :]).astype(BF16).reshape(bsz, tp, D_S5)
             for s in range(tq // tp)]
    yb_b = jnp.concatenate(parts, axis=1).reshape(m, D_S5)
    ya = ya_ref[...].reshape(m, D_SSD)
    return x_ref[...].reshape(m, D_MODEL) + _dot(jnp.concatenate([ya, yb_b], axis=1), w_ref[...])


def _outproj_kernel(x_ref, ya_ref, yb_ref, pt_ref, w_ref, g_ref, o_ref, *, bsz, tq, tp):
    x = _out_rows(x_ref, ya_ref, yb_ref, pt_ref, w_ref, bsz, tq, tp)
    o_ref[...] = _rmsnorm(x, g_ref[...]).reshape(bsz, tq, D_MODEL)


def _outin_kernel(x_ref, ya_ref, yb_ref, pt_ref, w_ref, g_ref, p_ref, wu_ref, wzb_ref,
                  o_ref, h_ref, u_ref, zb_ref, *, bsz, tq, tp):
    x = _out_rows(x_ref, ya_ref, yb_ref, pt_ref, w_ref, bsz, tq, tp)
    o_ref[...] = x.reshape(bsz, tq, D_MODEL)
    _in_rows(x, g_ref, p_ref, wu_ref, wzb_ref, h_ref, u_ref, zb_ref, bsz, tq, tp)


def _outproj(x, ya, yb, perm_t, w, g, tq, tp, nxt=None):
    bsz, length, _ = x.shape
    m = bsz * tq
    bmaj = pl.BlockSpec((bsz, tq, D_MODEL), lambda i: (0, i, 0))
    tmaj = pl.BlockSpec((m, D_S5), lambda i: (i, 0))
    full = lambda shape: pl.BlockSpec(shape, lambda i: (0,) * len(shape))
    args = [x, ya, yb, perm_t, w, g]
    in_specs = [bmaj, bmaj, tmaj, full(perm_t.shape), full(w.shape), full(g.shape)]
    x_out = jax.ShapeDtypeStruct(x.shape, F32)
    if nxt is None:
        body, out_specs, out_shape = _outproj_kernel, bmaj, x_out
    else:
        args += list(nxt)
        in_specs += [full(a.shape) for a in nxt]
        body = _outin_kernel
        out_specs = [bmaj, bmaj, tmaj, tmaj]
        out_shape = [x_out, jax.ShapeDtypeStruct(x.shape, BF16)] \
            + [jax.ShapeDtypeStruct((length * bsz, D_S5), F32)] * 2
    return pl.pallas_call(
        functools.partial(body, bsz=bsz, tq=tq, tp=tp),
        grid=(length // tq,),
        in_specs=in_specs,
        out_specs=out_specs,
        out_shape=out_shape,
        compiler_params=pltpu.CompilerParams(
            dimension_semantics=("parallel",), vmem_limit_bytes=VMEM_LIMIT),
        name="outproj",
    )(*args)


def _prep_layer(l, norm_g, w_in, conv_w, conv_b, dt_bias, a_log, d_ssd, ssd_norm_g,
                lam_re, lam_im, log_dt, b_re, b_im, c_re, c_im, d_s5, w_glu, b_glu, w_out):
    w = w_in[l]
    o1 = D_SSD
    o2 = o1 + SSD_CONV_DIM
    o3 = o2 + SSD_HEADS
    o4 = o3 + D_S5
    wdt = jnp.zeros((D_MODEL, DT_PAD), F32).at[:, :SSD_HEADS].set(w[:, o2:o3])
    pad_h = lambda v: jnp.zeros((1, DT_PAD), F32).at[0, :SSD_HEADS].set(v)
    lam = lax.complex(lam_re[l], lam_im[l])
    lam_bar = jnp.exp(lam * jnp.exp(log_dt[l])[:, None])
    b_bar = ((lam_bar - 1.0) / lam)[..., None] * lax.complex(b_re[l], b_im[l])
    gpt = S5_KT // S5_GROUP_CH
    same_group = ((jnp.arange(S5_KT) // S5_GROUP_CH)[:, None]
                  == (jnp.arange(S5_NT) // S5_STATE)[None, :])

    def blockdiag_b(v):
        v = jnp.transpose(v, (0, 2, 1)).reshape(S5_NJ, S5_KT, S5_STATE)
        return jnp.where(same_group, jnp.tile(v, (1, 1, gpt)), 0.0).astype(BF16)

    def blockdiag_c(v):
        v = jnp.transpose(v, (0, 2, 1)).reshape(S5_NJ, S5_NT, S5_GROUP_CH)
        return jnp.where(same_group.T, jnp.tile(v, (1, 1, gpt)), 0.0).astype(BF16)

    return dict(
        norm_g=norm_g[l][None, :],
        wza=w[:, :o1].astype(BF16), wxbc=w[:, o1:o2].astype(BF16), wdt=wdt.astype(BF16),
        wu=w[:, o3:o4].astype(BF16), wzb=w[:, o4:].astype(BF16),
        cw=conv_w[l], cb=conv_b[l][None, :], dtb=pad_h(dt_bias[l]), alog=pad_h(a_log[l]),
        dexp=jnp.repeat(d_ssd[l], SSD_HEADDIM)[None, :], ng=ssd_norm_g[l][None, :],
        lr=jnp.real(lam_bar).reshape(1, S5_LANES), li=jnp.imag(lam_bar).reshape(1, S5_LANES),
        bre=blockdiag_b(jnp.real(b_bar)), bim=blockdiag_b(jnp.imag(b_bar)),
        cre=blockdiag_c(c_re[l]), cim=blockdiag_c(-c_im[l]),
        d5=d_s5[l][None, :], wg=w_glu[l].astype(BF16), bg=b_glu[l][None, :],
        wout=w_out[l].astype(BF16),
    )


def _trunk(x, states, layers, final_g, expand):
    bsz, length, _ = x.shape
    tl = _tiles(bsz, length)
    tq, tp = tl['tq'], tl['tp']
    r = jnp.arange(bsz * tp)
    perm = (((r % bsz) * tp + r // bsz)[:, None] == r[None, :]).astype(BF16)
    convs, res, ims = [], [], []
    ssd_acc = None
    h, u, zb = _inproj(x, layers[0]['norm_g'], perm, layers[0]['wu'], layers[0]['wzb'], tq, tp)
    for l, p in enumerate(layers):
        if states is None:
            conv_prev = h0 = h0r = h0i = None
        else:
            conv_s, ssd_s, s5re_s, s5im_s = states
            conv_prev = conv_s[l]
            h0 = ssd_s[l].reshape(bsz, D_SSD, SSD_STATE)
            h0r = s5re_s[l].reshape(bsz, S5_LANES)
            h0i = s5im_s[l].reshape(bsz, S5_LANES)
        ya, conv_new, ssd_acc = _ssd(h, p['wza'], p['wxbc'], p['wdt'], conv_prev, h0,
                                     p['cw'], p['cb'], p['dtb'], p['alog'], p['dexp'], p['ng'],
                                     expand, tl['q'], tl['bg'], l, ssd_acc)
        yb, hr, hi = _s5(u, zb, h0r, h0i, p['lr'], p['li'], p['bre'], p['bim'],
                         p['cre'], p['cim'], p['d5'], p['wg'], p['bg'], bsz, tl['m'])
        if l + 1 < len(layers):
            n = layers[l + 1]
            x, h, u, zb = _outproj(x, ya, yb, perm.T, p['wout'], n['norm_g'], tq, tp,
                                   nxt=(perm, n['wu'], n['wzb']))
        else:
            x = _outproj(x, ya, yb, perm.T, p['wout'], final_g, tq, tp)
        convs.append(conv_new)
        res.append(hr.reshape(bsz, S5_GROUPS, S5_STATE))
        ims.append(hi.reshape(bsz, S5_GROUPS, S5_STATE))
    ssds = ssd_acc.reshape(DEPTH, bsz, SSD_HEADS, SSD_HEADDIM, SSD_STATE)
    return x, jnp.stack(convs), ssds, jnp.stack(res), jnp.stack(ims)


def kernel(x_prompt, x_sample, state_ssd_conv, state_ssd, state_s5_re, state_s5_im, norm_g, w_in, conv_w, conv_b, dt_bias, a_log, d_ssd, ssd_norm_g, lam_re, lam_im, log_dt, b_re, b_im, c_re, c_im, d_s5, w_glu, b_glu, w_out, final_norm_g):
    layers = [_prep_layer(l, norm_g, w_in, conv_w, conv_b, dt_bias, a_log, d_ssd, ssd_norm_g,
                          lam_re, lam_im, log_dt, b_re, b_im, c_re, c_im, d_s5, w_glu, b_glu,
                          w_out) for l in range(DEPTH)]
    final_g = final_norm_g[None, :]
    e = (jnp.arange(DT_PAD)[:, None] == (jnp.arange(D_SSD) // SSD_HEADDIM)[None, :]).astype(BF16)
    expand = jnp.concatenate([e, e, e], axis=0)
    yp, conv_p, ssd_p, re_p, im_p = _trunk(x_prompt, None, layers, final_g, expand)
    ys, conv_s, ssd_s, re_s, im_s = _trunk(
        x_sample, (state_ssd_conv, state_ssd, state_s5_re, state_s5_im), layers, final_g, expand)
    return (yp, ys, conv_p, ssd_p, re_p, im_p, conv_s, ssd_s, re_s, im_s)
```
